```python
import jax, jax.numpy as jnp
from jax import lax
import numpy as np

D_MODEL = 2048
BATCH = 4
SEQ = 4096
DEPTH = 1

N_Q_HEADS = 16
N_KV_HEADS = 4
HEAD_DIM = 64
Q_GROUP = N_Q_HEADS // N_KV_HEADS
ATTN_WIDTH = N_Q_HEADS * HEAD_DIM
KV_WIDTH = N_KV_HEADS * HEAD_DIM
WINDOW = 128
BLOCK = 128
ROPE_THETA = 500000.0
ROT_DIM = HEAD_DIM // 4
POOL_WINDOWS = (2, 4, 8, 16)
N_POOL_GROUPS = len(POOL_WINDOWS)
POOL_WIDTH = D_MODEL // 2
POOL_GROUP = POOL_WIDTH // N_POOL_GROUPS
N_BRANCHES = 2
IN_SPLITS = (POOL_WIDTH, ATTN_WIDTH, KV_WIDTH, KV_WIDTH, D_MODEL, D_MODEL)
IN_WIDTH = sum(IN_SPLITS)
D_FF = 5504
N_SUBLAYERS = 3
LN_EPS = 1e-5
DN_ALPHA = (2 * DEPTH) ** 0.25
DN_BETA = (8 * DEPTH) ** -0.25

kernel_name = "hybrid_pool_swa_macaron_deepnorm_adaln"


def layer_norm(x, g, b):
    xf = x.astype(jnp.float32)
    mu = jnp.mean(xf, axis=-1, keepdims=True)
    var = jnp.mean(jnp.square(xf - mu), axis=-1, keepdims=True)
    y = (xf - mu) * lax.rsqrt(var + LN_EPS)
    return (y * g.astype(jnp.float32) + b.astype(jnp.float32)).astype(x.dtype)


def modulate(x, shift, scale):
    return x * (1.0 + scale[:, None, :]) + shift[:, None, :]


def swiglu(u, w_gu, w_down):
    a, b = jnp.split(u @ w_gu, 2, axis=-1)
    return (jax.nn.silu(a) * b) @ w_down


def rope_partial(t, cos, sin):
    half = ROT_DIM // 2
    t1 = t[..., :half]
    t2 = t[..., half:ROT_DIM]
    c = cos[None, :, None, :].astype(t.dtype)
    s = sin[None, :, None, :].astype(t.dtype)
    return jnp.concatenate([t1 * c - t2 * s, t2 * c + t1 * s, t[..., ROT_DIM:]], axis=-1)


def pool_mixer(xp, w_pool, pool_scale):
    B, S, _ = xp.shape
    groups = xp.reshape(B, S, N_POOL_GROUPS, POOL_GROUP)
    t1 = jnp.arange(S) + 1
    outs = []
    for gi, w in enumerate(POOL_WINDOWS):
        xg = groups[:, :, gi, :].astype(jnp.float32)
        cs = jnp.cumsum(xg, axis=1)
        lag = jnp.pad(cs, ((0, 0), (w, 0), (0, 0)))[:, :S]
        count = jnp.minimum(t1, w).astype(jnp.float32)[None, :, None]
        outs.append((cs - lag) / count - xg)
    pooled = jnp.stack(outs, axis=2).astype(xp.dtype)
    mixed = jnp.einsum('bsgc,gcd->bsgd', pooled, w_pool)
    return mixed.reshape(B, S, POOL_WIDTH) * pool_scale


def sliding_window_attention(q, k, v, sinks):
    B, S = q.shape[0], q.shape[1]
    nb = S // BLOCK
    qb = q.reshape(B, nb, BLOCK, N_KV_HEADS, Q_GROUP, HEAD_DIM)

    def with_prev(t):
        tb = t.reshape(B, nb, BLOCK, N_KV_HEADS, HEAD_DIM)
        prev = jnp.pad(tb[:, :-1], ((0, 0), (1, 0), (0, 0), (0, 0), (0, 0)))
        return jnp.concatenate([prev, tb], axis=2)

    kw = with_prev(k)
    vw = with_prev(v)
    s = jnp.einsum('bnqhgd,bnkhd->bnhgqk', qb, kw,
                   preferred_element_type=jnp.float32) * (HEAD_DIM ** -0.5)
    qi = jnp.arange(BLOCK)[:, None]
    kj = jnp.arange(2 * BLOCK)[None, :]
    diff = qi - kj + BLOCK
    kpos = jnp.arange(nb)[:, None, None] * BLOCK - BLOCK + kj[None]
    valid = (diff >= 0)[None] & (diff < WINDOW)[None] & (kpos >= 0)
    s = jnp.where(valid[None, :, None, None], s, -1e30)
    sink = sinks.astype(jnp.float32).reshape(1, 1, N_KV_HEADS, Q_GROUP, 1, 1)
    m = jnp.maximum(jnp.max(s, axis=-1, keepdims=True), sink)
    p = jnp.exp(s - m)
    probs = p / (jnp.sum(p, axis=-1, keepdims=True) + jnp.exp(sink - m))
    o = jnp.einsum('bnhgqk,bnkhd->bnqhgd', probs.astype(v.dtype), vw)
    return o.reshape(B, S, ATTN_WIDTH)


def setup_inputs(seed: int = 0) -> dict:
    key = jax.random.key(seed)
    ks = jax.random.split(key, 24)
    f32 = jnp.float32
    L, D = DEPTH, D_MODEL

    def nrm(k, shape, std):
        return jax.random.normal(k, shape, f32) * std

    x = jax.random.normal(ks[0], (BATCH, SEQ, D), f32)
    c = jax.random.normal(ks[1], (BATCH, D), f32)
    w_ada = nrm(ks[2], (L, D, N_SUBLAYERS * 3 * D), 0.2 * D ** -0.5)
    b_ada = nrm(ks[3], (L, N_SUBLAYERS * 3 * D), 0.01)
    ln_g = 1.0 + nrm(ks[4], (L, N_SUBLAYERS, D), 0.05)
    ln_b = nrm(ks[5], (L, N_SUBLAYERS, D), 0.01)
    w_ffn1_in = nrm(ks[6], (L, D, 2 * D_FF), DN_BETA * D ** -0.5)
    w_ffn1_out = nrm(ks[7], (L, D_FF, D), DN_BETA * D_FF ** -0.5)
    w_in = jnp.concatenate([
        nrm(ks[8], (L, D, POOL_WIDTH), D ** -0.5),
        nrm(ks[9], (L, D, ATTN_WIDTH), D ** -0.5),
        nrm(ks[10], (L, D, KV_WIDTH), D ** -0.5),
        nrm(ks[11], (L, D, KV_WIDTH), DN_BETA * D ** -0.5),
        nrm(ks[12], (L, D, N_BRANCHES * D), D ** -0.5),
    ], axis=-1)
    b_in = nrm(ks[13], (L, IN_WIDTH), 0.01)
    w_pool = nrm(ks[14], (L, N_POOL_GROUPS, POOL_GROUP, POOL_GROUP), POOL_GROUP ** -0.5)
    pool_scale = 1.0 + nrm(ks[15], (L, POOL_WIDTH), 0.1)
    sinks = nrm(ks[16], (L, N_Q_HEADS), 0.5)
    w_branch_a = nrm(ks[17], (L, POOL_WIDTH, D), DN_BETA * POOL_WIDTH ** -0.5)
    w_branch_b = nrm(ks[18], (L, ATTN_WIDTH, D), DN_BETA * ATTN_WIDTH ** -0.5)
    w_out = nrm(ks[19], (L, D, D), DN_BETA * D ** -0.5)
    w_ffn2_in = nrm(ks[20], (L, D, 2 * D_FF), DN_BETA * D ** -0.5)
    w_ffn2_out = nrm(ks[21], (L, D_FF, D), DN_BETA * D_FF ** -0.5)
    return {"x": x, "c": c, "w_ada": w_ada, "b_ada": b_ada, "ln_g": ln_g, "ln_b": ln_b,
            "w_ffn1_in": w_ffn1_in, "w_ffn1_out": w_ffn1_out, "w_in": w_in, "b_in": b_in,
            "w_pool": w_pool, "pool_scale": pool_scale, "sinks": sinks,
            "w_branch_a": w_branch_a, "w_branch_b": w_branch_b, "w_out": w_out,
            "w_ffn2_in": w_ffn2_in, "w_ffn2_out": w_ffn2_out}


def reference(x, c, w_ada, b_ada, ln_g, ln_b, w_ffn1_in, w_ffn1_out, w_in, b_in, w_pool,
              pool_scale, sinks, w_branch_a, w_branch_b, w_out, w_ffn2_in, w_ffn2_out):
    B, S, D = x.shape
    pos = jnp.arange(S, dtype=jnp.float32)
    inv_freq = ROPE_THETA ** (-jnp.arange(0, ROT_DIM, 2, dtype=jnp.float32) / ROT_DIM)
    ang = pos[:, None] * inv_freq[None, :]
    cos, sin = jnp.cos(ang), jnp.sin(ang)
    split_at = list(np.cumsum(IN_SPLITS)[:-1])
    c_act = jax.nn.silu(c)

    for l in range(DEPTH):
        mod = (c_act @ w_ada[l] + b_ada[l]).reshape(B, N_SUBLAYERS, 3, D)

        u = modulate(x, mod[:, 0, 0], mod[:, 0, 1])
        y = swiglu(u, w_ffn1_in[l], w_ffn1_out[l])
        x = layer_norm(DN_ALPHA * x + 0.5 * (1.0 + mod[:, 0, 2])[:, None, :] * y,
                       ln_g[l, 0], ln_b[l, 0])

        u = modulate(x, mod[:, 1, 0], mod[:, 1, 1])
        h = u @ w_in[l] + b_in[l]
        xp, q, k, v, gl_a, gl_b = jnp.split(h, split_at, axis=-1)
        q = rope_partial(q.reshape(B, S, N_Q_HEADS, HEAD_DIM), cos, sin)
        k = rope_partial(k.reshape(B, S, N_KV_HEADS, HEAD_DIM), cos, sin)
        v = v.reshape(B, S, N_KV_HEADS, HEAD_DIM)
        y_a = pool_mixer(xp, w_pool[l], pool_scale[l]) @ w_branch_a[l]
        y_b = sliding_window_attention(q, k, v, sinks[l]) @ w_branch_b[l]
        merged = jax.nn.sigmoid(gl_a) * y_a + jax.nn.sigmoid(gl_b) * y_b
        y = merged @ w_out[l]
        x = layer_norm(DN_ALPHA * x + (1.0 + mod[:, 1, 2])[:, None, :] * y,
                       ln_g[l, 1], ln_b[l, 1])

        u = modulate(x, mod[:, 2, 0], mod[:, 2, 1])
        y = swiglu(u, w_ffn2_in[l], w_ffn2_out[l])
        x = layer_norm(DN_ALPHA * x + 0.5 * (1.0 + mod[:, 2, 2])[:, None, :] * y,
                       ln_g[l, 2], ln_b[l, 2])
    return x
```

```python
import functools

import jax
import jax.numpy as jnp
import numpy as np
from jax import lax
from jax.experimental import pallas as pl
from jax.experimental.pallas import tpu as pltpu

F32 = jnp.float32
BF16 = jnp.bfloat16

D_MODEL = 2048
N_Q_HEADS = 16
N_KV_HEADS = 4
HEAD_DIM = 64
Q_GROUP = N_Q_HEADS // N_KV_HEADS
ATTN_WIDTH = N_Q_HEADS * HEAD_DIM
KV_WIDTH = N_KV_HEADS * HEAD_DIM
WINDOW = 128
ROPE_THETA = 500000.0
ROT_DIM = HEAD_DIM // 4
ROT_HALF = ROT_DIM // 2
POOL_WINDOWS = (2, 4, 8, 16)
POOL_WIDTH = D_MODEL // 2
POOL_GROUP = POOL_WIDTH // len(POOL_WINDOWS)
POOL_HALO = 16
GATE_WIDTH = D_MODEL
IN_WIDTH = POOL_WIDTH + ATTN_WIDTH + 2 * KV_WIDTH + 2 * GATE_WIDTH
D_FF = 5504
N_SUBLAYERS = 3
LN_EPS = 1e-5
MASK_VALUE = -1e30

LANES = 128
VMEM_LIMIT = 56 * 1024 * 1024
FFN_TM = 512
FFN_TF = 512
D_FF_PAD = ((D_FF + FFN_TF - 1) // FFN_TF) * FFN_TF
PROJ_TM = 512
PROJ_TN = 512
MERGE_TM = 256
ROW_CHUNK = 128
MOD_TN = 1024
MOD_ROWS = 8


def _params(*sem):
    return pltpu.CompilerParams(dimension_semantics=sem, vmem_limit_bytes=VMEM_LIMIT)


def _resident(shape, index_map):
    return pl.BlockSpec(shape, index_map, pipeline_mode=pl.Buffered(1))


def _layer_norm(z, g, b):
    mu = jnp.mean(z, axis=-1, keepdims=True)
    zc = z - mu
    var = jnp.mean(zc * zc, axis=-1, keepdims=True)
    return zc * lax.rsqrt(var + LN_EPS) * g + b


def _mod_kernel(c_ref, w_ref, b_ref, o_ref):
    c = c_ref[...]
    c_act = (c * jax.nn.sigmoid(c)).astype(BF16)
    o_ref[...] = jnp.dot(c_act, w_ref[...].astype(BF16),
                         preferred_element_type=F32) + b_ref[...]


def _modulation(c_pad, w_ada, b_ada):
    n = w_ada.shape[1]
    return pl.pallas_call(
        _mod_kernel,
        grid=(n // MOD_TN,),
        in_specs=[
            pl.BlockSpec((MOD_ROWS, D_MODEL), lambda j: (0, 0)),
            pl.BlockSpec((D_MODEL, MOD_TN), lambda j: (0, j)),
            pl.BlockSpec((1, MOD_TN), lambda j: (0, j)),
        ],
        out_specs=pl.BlockSpec((MOD_ROWS, MOD_TN), lambda j: (0, j)),
        out_shape=jax.ShapeDtypeStruct((MOD_ROWS, n), F32),
        compiler_params=_params("arbitrary"),
        name="adaln_mod",
    )(c_pad, w_ada, b_ada.reshape(1, n))


def _ffn_kernel(x_ref, shift_ref, scale_ref, gate_ref, g_ref, b_ref,
                wg_ref, wu_ref, wd_ref, o_ref, u_scr, acc_scr, *, alpha):
    j = pl.program_id(1)
    n_chunks = FFN_TM // ROW_CHUNK

    @pl.when(j == 0)
    def _():
        scale1 = 1.0 + scale_ref[0]
        shift = shift_ref[0]

        def body(r, carry):
            rows = pl.ds(pl.multiple_of(r * ROW_CHUNK, ROW_CHUNK), ROW_CHUNK)
            u_scr[rows, :] = (x_ref[rows, :] * scale1 + shift).astype(BF16)
            return carry

        lax.fori_loop(0, n_chunks, body, 0)

    u = u_scr[...]
    a = jnp.dot(u, wg_ref[...], preferred_element_type=F32)
    b = jnp.dot(u, wu_ref[...], preferred_element_type=F32)
    h = (a * jax.nn.sigmoid(a) * b).astype(BF16)
    part = jnp.dot(h, wd_ref[...], preferred_element_type=F32)

    @pl.when(j == 0)
    def _():
        acc_scr[...] = part

    @pl.when(j > 0)
    def _():
        acc_scr[...] += part

    @pl.when(j == pl.num_programs(1) - 1)
    def _():
        gate_half = 0.5 * (1.0 + gate_ref[0])
        g = g_ref[...]
        beta = b_ref[...]

        def body(r, carry):
            rows = pl.ds(pl.multiple_of(r * ROW_CHUNK, ROW_CHUNK), ROW_CHUNK)
            z = alpha * x_ref[rows, :] + gate_half * acc_scr[rows, :]
            o_ref[rows, :] = _layer_norm(z, g, beta)
            return carry

        lax.fori_loop(0, n_chunks, body, 0)


def _ffn_sublayer(x, shift, scale, gate, ln_g, ln_b, w_g, w_u, w_d, *, seq, alpha):
    m = x.shape[0]
    tiles_per_seq = seq // FFN_TM
    row = lambda i, j: (i, 0)
    per_batch = lambda i, j: (i // tiles_per_seq, 0, 0)
    const = lambda i, j: (0, 0)
    return pl.pallas_call(
        functools.partial(_ffn_kernel, alpha=alpha),
        grid=(m // FFN_TM, D_FF_PAD // FFN_TF),
        in_specs=[
            pl.BlockSpec((FFN_TM, D_MODEL), row),
            pl.BlockSpec((1, 1, D_MODEL), per_batch),
            pl.BlockSpec((1, 1, D_MODEL), per_batch),
            pl.BlockSpec((1, 1, D_MODEL), per_batch),
            pl.BlockSpec((1, D_MODEL), const),
            pl.BlockSpec((1, D_MODEL), const),
            pl.BlockSpec((D_MODEL, FFN_TF), lambda i, j: (0, j)),
            pl.BlockSpec((D_MODEL, FFN_TF), lambda i, j: (0, j)),
            pl.BlockSpec((FFN_TF, D_MODEL), lambda i, j: (j, 0)),
        ],
        out_specs=pl.BlockSpec((FFN_TM, D_MODEL), row),
        out_shape=jax.ShapeDtypeStruct((m, D_MODEL), F32),
        scratch_shapes=[
            pltpu.VMEM((FFN_TM, D_MODEL), BF16),
            pltpu.VMEM((FFN_TM, D_MODEL), F32),
        ],
        compiler_params=_params("parallel", "arbitrary"),
        name="ffn_sublayer",
    )(x, shift, scale, gate, ln_g, ln_b, w_g, w_u, w_d)


def _rope(t, cosf, sina, sinb):
    up = pltpu.roll(t, LANES - ROT_HALF, axis=1)
    dn = pltpu.roll(t, ROT_HALF, axis=1)
    return t * cosf + up * sina + dn * sinb


def _rope_wide(t, cosf, sina, sinb):
    slabs = [_rope(t[:, s:s + LANES], cosf, sina, sinb)
             for s in range(0, t.shape[1], LANES)]
    return jnp.concatenate(slabs, axis=1)


def _proj_kernel(x_ref, shift_ref, scale_ref, w_ref, b_ref, cos_ref, sina_ref, sinb_ref,
                 xp_ref, q_ref, k_ref, v_ref, sa_ref, sb_ref, u_scr):
    scale1 = 1.0 + scale_ref[0]
    shift = shift_ref[0]

    def body(r, carry):
        rows = pl.ds(pl.multiple_of(r * ROW_CHUNK, ROW_CHUNK), ROW_CHUNK)
        u_scr[rows, :] = (x_ref[rows, :] * scale1 + shift).astype(BF16)
        return carry

    lax.fori_loop(0, PROJ_TM // ROW_CHUNK, body, 0)

    u = u_scr[...]
    cosf = cos_ref[...]
    sina = sina_ref[...]
    sinb = sinb_ref[...]
    q_off = POOL_WIDTH
    kv_off = q_off + ATTN_WIDTH
    ga_off = kv_off + 2 * KV_WIDTH
    gb_off = ga_off + GATE_WIDTH
    for col in range(0, IN_WIDTH, PROJ_TN):
        cols = slice(col, col + PROJ_TN)
        h = jnp.dot(u, w_ref[:, cols], preferred_element_type=F32) + b_ref[:, cols]
        if col < q_off:
            xp_ref[:, cols] = h
        elif col < kv_off:
            q = _rope_wide(h, cosf, sina, sinb) * (HEAD_DIM ** -0.5)
            q_ref[:, col - q_off:col - q_off + PROJ_TN] = q.astype(BF16)
        elif col < ga_off:
            k_ref[...] = _rope_wide(h[:, :KV_WIDTH], cosf, sina, sinb).astype(BF16)
            v_ref[...] = h[:, KV_WIDTH:].astype(BF16)
        elif col < gb_off:
            sa_ref[:, col - ga_off:col - ga_off + PROJ_TN] = jax.nn.sigmoid(h).astype(BF16)
        else:
            sb_ref[:, col - gb_off:col - gb_off + PROJ_TN] = jax.nn.sigmoid(h).astype(BF16)


def _input_projection(x, shift, scale, w_in, b_in, cosf, sina, sinb, *, seq):
    m = x.shape[0]
    tiles_per_seq = seq // PROJ_TM
    row = lambda i: (i, 0)
    per_batch = lambda i: (i // tiles_per_seq, 0, 0)
    const = lambda i: (0, 0)
    pos = lambda i: (i % tiles_per_seq, 0)
    out_widths = (POOL_WIDTH, ATTN_WIDTH, KV_WIDTH, KV_WIDTH, GATE_WIDTH, GATE_WIDTH)
    out_dtypes = (F32, BF16, BF16, BF16, BF16, BF16)
    return pl.pallas_call(
        _proj_kernel,
        grid=(m // PROJ_TM,),
        in_specs=[
            pl.BlockSpec((PROJ_TM, D_MODEL), row),
            pl.BlockSpec((1, 1, D_MODEL), per_batch),
            pl.BlockSpec((1, 1, D_MODEL), per_batch),
            _resident((D_MODEL, IN_WIDTH), const),
            _resident((1, IN_WIDTH), const),
            pl.BlockSpec((PROJ_TM, LANES), pos),
            pl.BlockSpec((PROJ_TM, LANES), pos),
            pl.BlockSpec((PROJ_TM, LANES), pos),
        ],
        out_specs=[pl.BlockSpec((PROJ_TM, w), row) for w in out_widths],
        out_shape=[jax.ShapeDtypeStruct((m, w), dt) for w, dt in zip(out_widths, out_dtypes)],
        scratch_shapes=[pltpu.VMEM((PROJ_TM, D_MODEL), BF16)],
        compiler_params=_params("parallel"),
        name="mix_in_proj",
    )(x, shift, scale, w_in, b_in, cosf, sina, sinb)


def _attn_kernel(sinks_ref, q_ref, kp_ref, kc_ref, vp_ref, vc_ref, o_ref):
    i = pl.program_id(1)
    q = q_ref[...]
    kk = jnp.concatenate([kp_ref[...], kc_ref[...]], axis=0)
    vv = jnp.concatenate([vp_ref[...], vc_ref[...]], axis=0)
    qi = lax.broadcasted_iota(jnp.int32, (WINDOW, 2 * WINDOW), 0)
    kj = lax.broadcasted_iota(jnp.int32, (WINDOW, 2 * WINDOW), 1)
    diff = qi - kj + WINDOW
    valid = (diff >= 0) & (diff < WINDOW) & ((kj >= WINDOW) | (i > 0))
    outs = []
    for head in range(N_Q_HEADS):
        kv = head // Q_GROUP
        kg = kk[:, kv * HEAD_DIM:(kv + 1) * HEAD_DIM]
        vg = vv[:, kv * HEAD_DIM:(kv + 1) * HEAD_DIM]
        qh = q[:, head * HEAD_DIM:(head + 1) * HEAD_DIM]
        s = lax.dot_general(qh, kg, (((1,), (1,)), ((), ())), preferred_element_type=F32)
        s = jnp.where(valid, s, MASK_VALUE)
        sink = sinks_ref[head]
        m = jnp.maximum(jnp.max(s, axis=-1, keepdims=True), sink)
        p = jnp.exp(s - m)
        denom = jnp.sum(p, axis=-1, keepdims=True) + jnp.exp(sink - m)
        o = jnp.dot(p.astype(BF16), vg, preferred_element_type=F32) / denom
        outs.append(o.astype(BF16))
    o_ref[...] = jnp.concatenate(outs, axis=1)


def _attention(q, k, v, sinks, *, batch, seq):
    m = q.shape[0]
    nb = seq // WINDOW
    cur = lambda b, i: (b * nb + i, 0)
    prev = lambda b, i: (b * nb + jnp.maximum(i - 1, 0), 0)
    return pl.pallas_call(
        _attn_kernel,
        grid=(batch, nb),
        in_specs=[
            pl.BlockSpec(memory_space=pltpu.SMEM),
            pl.BlockSpec((WINDOW, ATTN_WIDTH), cur),
            pl.BlockSpec((WINDOW, KV_WIDTH), prev),
            pl.BlockSpec((WINDOW, KV_WIDTH), cur),
            pl.BlockSpec((WINDOW, KV_WIDTH), prev),
            pl.BlockSpec((WINDOW, KV_WIDTH), cur),
        ],
        out_specs=pl.BlockSpec((WINDOW, ATTN_WIDTH), cur),
        out_shape=jax.ShapeDtypeStruct((m, ATTN_WIDTH), BF16),
        compiler_params=_params("parallel", "parallel"),
        name="swa_attention",
    )(sinks, q, k, k, v, v)


def _merge_kernel(xp_ref, xprev_ref, attn_ref, sa_ref, sb_ref, x_ref, gate_ref, g_ref, b_ref,
                  wpool_ref, pscale_ref, wa_ref, wb_ref, wout_ref, o_ref, *, alpha, tiles_per_seq):
    tile_in_seq = pl.program_id(0) % tiles_per_seq
    xp = xp_ref[...]
    halo = jnp.where(tile_in_seq > 0, xprev_ref[...], 0.0)
    xfull = jnp.concatenate([halo, xp], axis=0)
    t1 = tile_in_seq * MERGE_TM + lax.broadcasted_iota(jnp.int32, (MERGE_TM, 1), 0) + 1
    mixed = []
    for gi, w in enumerate(POOL_WINDOWS):
        cols = slice(gi * POOL_GROUP, (gi + 1) * POOL_GROUP)
        s = xfull[:, cols]
        span = 1
        while span < w:
            s = s + pltpu.roll(s, span, axis=0)
            span *= 2
        count = jnp.minimum(t1, w).astype(F32)
        pooled = s[POOL_HALO:] / count - xp[:, cols]
        mg = jnp.dot(pooled.astype(BF16), wpool_ref[gi], preferred_element_type=F32)
        mixed.append((mg * pscale_ref[:, cols]).astype(BF16))
    mixed = jnp.concatenate(mixed, axis=1)
    y_a = jnp.dot(mixed, wa_ref[...], preferred_element_type=F32)
    y_b = jnp.dot(attn_ref[...], wb_ref[...], preferred_element_type=F32)
    merged = (sa_ref[...].astype(F32) * y_a + sb_ref[...].astype(F32) * y_b).astype(BF16)
    y = jnp.dot(merged, wout_ref[...], preferred_element_type=F32)
    z = alpha * x_ref[...] + (1.0 + gate_ref[0]) * y
    o_ref[...] = _layer_norm(z, g_ref[...], b_ref[...])


def _merge_sublayer(xp, attn, sa, sb, x, gate, ln_g, ln_b, w_pool, pool_scale,
                    w_a, w_b, w_out, *, seq, alpha):
    m = x.shape[0]
    tiles_per_seq = seq // MERGE_TM
    halo_blocks = MERGE_TM // POOL_HALO
    row = lambda i: (i, 0)
    prev_rows = lambda i: (jnp.maximum(i * halo_blocks - 1, 0), 0)
    per_batch = lambda i: (i // tiles_per_seq, 0, 0)
    const = lambda i: (0, 0)
    const3 = lambda i: (0, 0, 0)
    return pl.pallas_call(
        functools.partial(_merge_kernel, alpha=alpha, tiles_per_seq=tiles_per_seq),
        grid=(m // MERGE_TM,),
        in_specs=[
            pl.BlockSpec((MERGE_TM, POOL_WIDTH), row),
            pl.BlockSpec((POOL_HALO, POOL_WIDTH), prev_rows),
            pl.BlockSpec((MERGE_TM, ATTN_WIDTH), row),
            pl.BlockSpec((MERGE_TM, GATE_WIDTH), row),
            pl.BlockSpec((MERGE_TM, GATE_WIDTH), row),
            pl.BlockSpec((MERGE_TM, D_MODEL), row),
            pl.BlockSpec((1, 1, D_MODEL), per_batch),
            pl.BlockSpec((1, D_MODEL), const),
            pl.BlockSpec((1, D_MODEL), const),
            _resident((len(POOL_WINDOWS), POOL_GROUP, POOL_GROUP), const3),
            _resident((1, POOL_WIDTH), const),
            _resident((POOL_WIDTH, D_MODEL), const),
            _resident((ATTN_WIDTH, D_MODEL), const),
            _resident((D_MODEL, D_MODEL), const),
        ],
        out_specs=pl.BlockSpec((MERGE_TM, D_MODEL), row),
        out_shape=jax.ShapeDtypeStruct((m, D_MODEL), F32),
        compiler_params=_params("parallel"),
        name="mix_merge",
    )(xp, xp, attn, sa, sb, x, gate, ln_g, ln_b, w_pool, pool_scale, w_a, w_b, w_out)


def _rope_tables(seq):
    pos = jnp.arange(seq, dtype=F32)
    inv_freq = ROPE_THETA ** (-jnp.arange(0, ROT_DIM, 2, dtype=F32) / ROT_DIM)
    ang = pos[:, None] * inv_freq[None, :]
    cos, sin = jnp.cos(ang), jnp.sin(ang)
    d = np.arange(LANES) % HEAD_DIM
    idx = d % ROT_HALF
    lo = jnp.asarray(d < ROT_HALF)
    hi = jnp.asarray((d >= ROT_HALF) & (d < ROT_DIM))
    cosf = jnp.where(lo | hi, cos[:, idx], 1.0)
    sina = jnp.where(lo, -sin[:, idx], 0.0)
    sinb = jnp.where(hi, sin[:, idx], 0.0)
    return cosf, sina, sinb


def _ffn_weights(w_in, w_out):
    pad = D_FF_PAD - D_FF
    w_g = jnp.pad(w_in[:, :D_FF], ((0, 0), (0, pad))).astype(BF16)
    w_u = jnp.pad(w_in[:, D_FF:], ((0, 0), (0, pad))).astype(BF16)
    w_d = jnp.pad(w_out, ((0, pad), (0, 0))).astype(BF16)
    return w_g, w_u, w_d


def kernel(x, c, w_ada, b_ada, ln_g, ln_b, w_ffn1_in, w_ffn1_out, w_in, b_in, w_pool,
           pool_scale, sinks, w_branch_a, w_branch_b, w_out, w_ffn2_in, w_ffn2_out):
    batch, seq, d = x.shape
    depth = w_ada.shape[0]
    assert d == D_MODEL and batch <= MOD_ROWS
    assert seq % FFN_TM == 0 and seq % PROJ_TM == 0 and seq % MERGE_TM == 0 and seq % WINDOW == 0
    alpha = (2 * depth) ** 0.25
    cosf, sina, sinb = _rope_tables(seq)
    c_pad = jnp.pad(c, ((0, MOD_ROWS - batch), (0, 0)))
    h = x.reshape(batch * seq, d)

    for l in range(depth):
        mod = _modulation(c_pad, w_ada[l], b_ada[l])[:batch]
        mod = mod.reshape(batch, N_SUBLAYERS, 3, 1, d)
        shift = lambda s: mod[:, s, 0]
        scale = lambda s: mod[:, s, 1]
        gate = lambda s: mod[:, s, 2]
        g = lambda s: ln_g[l, s].reshape(1, d)
        b = lambda s: ln_b[l, s].reshape(1, d)

        w_g, w_u, w_d = _ffn_weights(w_ffn1_in[l], w_ffn1_out[l])
        h = _ffn_sublayer(h, shift(0), scale(0), gate(0), g(0), b(0), w_g, w_u, w_d,
                          seq=seq, alpha=alpha)

        xp, q, k, v, sa, sb = _input_projection(
            h, shift(1), scale(1), w_in[l].astype(BF16), b_in[l].reshape(1, IN_WIDTH),
            cosf, sina, sinb, seq=seq)
        attn = _attention(q, k, v, sinks[l], batch=batch, seq=seq)
        h = _merge_sublayer(
            xp, attn, sa, sb, h, gate(1), g(1), b(1), w_pool[l].astype(BF16),
            pool_scale[l].reshape(1, POOL_WIDTH), w_branch_a[l].astype(BF16),
            w_branch_b[l].astype(BF16), w_out[l].astype(BF16), seq=seq, alpha=alpha)

        w_g, w_u, w_d = _ffn_weights(w_ffn2_in[l], w_ffn2_out[l])
        h = _ffn_sublayer(h, shift(2), scale(2), gate(2), g(2), b(2), w_g, w_u, w_d,
                          seq=seq, alpha=alpha)
    return h.reshape(batch, seq, d)
```

```python
import functools

import jax
import jax.numpy as jnp
import numpy as np
from jax import lax
from jax.experimental import pallas as pl
from jax.experimental.pallas import tpu as pltpu

F32 = jnp.float32
BF16 = jnp.bfloat16

D_MODEL = 2048
N_Q_HEADS = 16
N_KV_HEADS = 4
HEAD_DIM = 64
Q_GROUP = N_Q_HEADS // N_KV_HEADS
ATTN_WIDTH = N_Q_HEADS * HEAD_DIM
KV_WIDTH = N_KV_HEADS * HEAD_DIM
WINDOW = 128
ROPE_THETA = 500000.0
ROT_DIM = HEAD_DIM // 4
ROT_HALF = ROT_DIM // 2
POOL_WINDOWS = (2, 4, 8, 16)
POOL_WIDTH = D_MODEL // 2
POOL_GROUP = POOL_WIDTH // len(POOL_WINDOWS)
POOL_HALO = 16
GATE_WIDTH = D_MODEL
IN_WIDTH = POOL_WIDTH + ATTN_WIDTH + 2 * KV_WIDTH + 2 * GATE_WIDTH
D_FF = 5504
N_SUBLAYERS = 3
LN_EPS = 1e-5
MASK_VALUE = -1e30

LANES = 128
VMEM_LIMIT = 60 * 1024 * 1024
FFN_TM = 1024
FFN_MC = 512
FFN_TF = 512
D_FF_PAD = ((D_FF + FFN_TF - 1) // FFN_TF) * FFN_TF
CAST_ROWS_IN = 64
CAST_ROWS_OUT = 128
assert D_FF % LANES == 0 and D_FF % CAST_ROWS_OUT == 0 and D_FF_PAD % CAST_ROWS_OUT == 0
PROJ_TM = 512
PROJ_TN = 512
ATTN_TQ = 512
MERGE_TM = 256
ROW_CHUNK = 128
MOD_TN = 1024
MOD_ROWS = 8


def _params(*sem):
    return pltpu.CompilerParams(dimension_semantics=sem, vmem_limit_bytes=VMEM_LIMIT)


def _resident(shape, index_map):
    return pl.BlockSpec(shape, index_map, pipeline_mode=pl.Buffered(1))


def _layer_norm(z, g, b):
    mu = jnp.mean(z, axis=-1, keepdims=True)
    zc = z - mu
    var = jnp.mean(zc * zc, axis=-1, keepdims=True)
    return zc * lax.rsqrt(var + LN_EPS) * g + b


def _mod_kernel(c_ref, w_ref, b_ref, o_ref):
    c = c_ref[...]
    c_act = (c * jax.nn.sigmoid(c)).astype(BF16)
    o_ref[...] = jnp.dot(c_act, w_ref[...].astype(BF16),
                         preferred_element_type=F32) + b_ref[...]


def _modulation(c_pad, w_ada, b_ada):
    n = w_ada.shape[1]
    return pl.pallas_call(
        _mod_kernel,
        grid=(n // MOD_TN,),
        in_specs=[
            pl.BlockSpec((MOD_ROWS, D_MODEL), lambda j: (0, 0)),
            pl.BlockSpec((D_MODEL, MOD_TN), lambda j: (0, j)),
            pl.BlockSpec((1, MOD_TN), lambda j: (0, j)),
        ],
        out_specs=pl.BlockSpec((MOD_ROWS, MOD_TN), lambda j: (0, j)),
        out_shape=jax.ShapeDtypeStruct((MOD_ROWS, n), F32),
        compiler_params=_params("arbitrary"),
        name="adaln_mod",
    )(c_pad, w_ada, b_ada.reshape(1, n))


def _cast_in_kernel(w_ref, o_ref):
    o_ref[0, :, :D_FF] = w_ref[:, :D_FF].astype(BF16)
    o_ref[1, :, :D_FF] = w_ref[:, D_FF:].astype(BF16)
    o_ref[:, :, D_FF:] = jnp.zeros((2, CAST_ROWS_IN, D_FF_PAD - D_FF), BF16)


def _cast_out_kernel(w_ref, o_ref):
    j = pl.program_id(0)

    @pl.when(j < D_FF // CAST_ROWS_OUT)
    def _():
        o_ref[...] = w_ref[...].astype(BF16)

    @pl.when(j >= D_FF // CAST_ROWS_OUT)
    def _():
        o_ref[...] = jnp.zeros_like(o_ref)


def _ffn_weights(w_in, w_out):
    w_gu = pl.pallas_call(
        _cast_in_kernel,
        grid=(D_MODEL // CAST_ROWS_IN,),
        in_specs=[pl.BlockSpec((CAST_ROWS_IN, 2 * D_FF), lambda i: (i, 0))],
        out_specs=pl.BlockSpec((2, CAST_ROWS_IN, D_FF_PAD), lambda i: (0, i, 0)),
        out_shape=jax.ShapeDtypeStruct((2, D_MODEL, D_FF_PAD), BF16),
        compiler_params=_params("parallel"),
        name="cast_ffn_in",
    )(w_in)
    last_in = D_FF // CAST_ROWS_OUT - 1
    w_d = pl.pallas_call(
        _cast_out_kernel,
        grid=(D_FF_PAD // CAST_ROWS_OUT,),
        in_specs=[pl.BlockSpec((CAST_ROWS_OUT, D_MODEL), lambda j: (jnp.minimum(j, last_in), 0))],
        out_specs=pl.BlockSpec((CAST_ROWS_OUT, D_MODEL), lambda j: (j, 0)),
        out_shape=jax.ShapeDtypeStruct((D_FF_PAD, D_MODEL), BF16),
        compiler_params=_params("parallel"),
        name="cast_ffn_out",
    )(w_out)
    return w_gu, w_d


def _ffn_accumulate(o_ref, u_scr, wg, wu, wd):
    for r in range(FFN_TM // FFN_MC):
        rows = slice(r * FFN_MC, (r + 1) * FFN_MC)
        u = u_scr[rows, :]
        a = jnp.dot(u, wg, preferred_element_type=F32)
        b = jnp.dot(u, wu, preferred_element_type=F32)
        h = (a * jax.nn.sigmoid(a) * b).astype(BF16)
        o_ref[rows, :] += jnp.dot(h, wd, preferred_element_type=F32)


def _ffn_kernel(x_ref, shift_ref, scale_ref, gate_ref, g_ref, b_ref, wg_ref, wu_ref, wd_ref,
                o_ref, u_scr, *, alpha):
    j = pl.program_id(1)
    n_chunks = FFN_TM // ROW_CHUNK

    @pl.when(j == 0)
    def _():
        scale1 = 1.0 + scale_ref[0]
        shift = shift_ref[0]

        def body(r, carry):
            rows = pl.ds(pl.multiple_of(r * ROW_CHUNK, ROW_CHUNK), ROW_CHUNK)
            u_scr[rows, :] = (x_ref[rows, :] * scale1 + shift).astype(BF16)
            o_ref[rows, :] = jnp.zeros((ROW_CHUNK, D_MODEL), F32)
            return carry

        lax.fori_loop(0, n_chunks, body, 0)

    _ffn_accumulate(o_ref, u_scr, wg_ref[...], wu_ref[...], wd_ref[...])

    @pl.when(j == pl.num_programs(1) - 1)
    def _():
        gate_half = 0.5 * (1.0 + gate_ref[0])
        g = g_ref[...]
        beta = b_ref[...]

        def body(r, carry):
            rows = pl.ds(pl.multiple_of(r * ROW_CHUNK, ROW_CHUNK), ROW_CHUNK)
            z = alpha * x_ref[rows, :] + gate_half * o_ref[rows, :]
            o_ref[rows, :] = _layer_norm(z, g, beta)
            return carry

        lax.fori_loop(0, n_chunks, body, 0)


def _ffn_sublayer(x, shift, scale, gate, ln_g, ln_b, w_gu, w_d, *, seq, alpha):
    m = x.shape[0]
    tiles_per_seq = seq // FFN_TM
    row = lambda i, j: (i, 0)
    per_batch = lambda i, j: (i // tiles_per_seq, 0, 0)
    const = lambda i, j: (0, 0)
    return pl.pallas_call(
        functools.partial(_ffn_kernel, alpha=alpha),
        grid=(m // FFN_TM, D_FF_PAD // FFN_TF),
        in_specs=[
            pl.BlockSpec((FFN_TM, D_MODEL), row),
            pl.BlockSpec((1, 1, D_MODEL), per_batch),
            pl.BlockSpec((1, 1, D_MODEL), per_batch),
            pl.BlockSpec((1, 1, D_MODEL), per_batch),
            pl.BlockSpec((1, D_MODEL), const),
            pl.BlockSpec((1, D_MODEL), const),
            pl.BlockSpec((None, D_MODEL, FFN_TF), lambda i, j: (0, 0, j)),
            pl.BlockSpec((None, D_MODEL, FFN_TF), lambda i, j: (1, 0, j)),
            pl.BlockSpec((FFN_TF, D_MODEL), lambda i, j: (j, 0)),
        ],
        out_specs=pl.BlockSpec((FFN_TM, D_MODEL), row),
        out_shape=jax.ShapeDtypeStruct((m, D_MODEL), F32),
        scratch_shapes=[pltpu.VMEM((FFN_TM, D_MODEL), BF16)],
        compiler_params=_params("parallel", "arbitrary"),
        name="ffn_sublayer",
    )(x, shift, scale, gate, ln_g, ln_b, w_gu, w_gu, w_d)


def _rope(t, cosf, sina, sinb):
    up = pltpu.roll(t, LANES - ROT_HALF, axis=1)
    dn = pltpu.roll(t, ROT_HALF, axis=1)
    return t * cosf + up * sina + dn * sinb


def _rope_wide(t, cosf, sina, sinb):
    slabs = [_rope(t[:, s:s + LANES], cosf, sina, sinb)
             for s in range(0, t.shape[1], LANES)]
    return jnp.concatenate(slabs, axis=1)


def _proj_kernel(x_ref, shift_ref, scale_ref, w_ref, b_ref, cos_ref, sina_ref, sinb_ref,
                 xp_ref, q_ref, k_ref, v_ref, sa_ref, sb_ref, u_scr):
    scale1 = 1.0 + scale_ref[0]
    shift = shift_ref[0]

    def body(r, carry):
        rows = pl.ds(pl.multiple_of(r * ROW_CHUNK, ROW_CHUNK), ROW_CHUNK)
        u_scr[rows, :] = (x_ref[rows, :] * scale1 + shift).astype(BF16)
        return carry

    lax.fori_loop(0, PROJ_TM // ROW_CHUNK, body, 0)

    u = u_scr[...]
    cosf = cos_ref[...]
    sina = sina_ref[...]
    sinb = sinb_ref[...]
    q_off = POOL_WIDTH
    kv_off = q_off + ATTN_WIDTH
    ga_off = kv_off + 2 * KV_WIDTH
    gb_off = ga_off + GATE_WIDTH
    for col in range(0, IN_WIDTH, PROJ_TN):
        cols = slice(col, col + PROJ_TN)
        h = jnp.dot(u, w_ref[:, cols], preferred_element_type=F32) + b_ref[:, cols]
        if col < q_off:
            xp_ref[:, cols] = h
        elif col < kv_off:
            q = _rope_wide(h, cosf, sina, sinb) * (HEAD_DIM ** -0.5)
            q_ref[:, col - q_off:col - q_off + PROJ_TN] = q.astype(BF16)
        elif col < ga_off:
            k_ref[...] = _rope_wide(h[:, :KV_WIDTH], cosf, sina, sinb).astype(BF16)
            v_ref[...] = h[:, KV_WIDTH:].astype(BF16)
        elif col < gb_off:
            sa_ref[:, col - ga_off:col - ga_off + PROJ_TN] = jax.nn.sigmoid(h).astype(BF16)
        else:
            sb_ref[:, col - gb_off:col - gb_off + PROJ_TN] = jax.nn.sigmoid(h).astype(BF16)


def _input_projection(x, shift, scale, w_in, b_in, cosf, sina, sinb, *, seq):
    m = x.shape[0]
    tiles_per_seq = seq // PROJ_TM
    row = lambda i: (i, 0)
    per_batch = lambda i: (i // tiles_per_seq, 0, 0)
    const = lambda i: (0, 0)
    pos = lambda i: (i % tiles_per_seq, 0)
    out_widths = (POOL_WIDTH, ATTN_WIDTH, KV_WIDTH, KV_WIDTH, GATE_WIDTH, GATE_WIDTH)
    out_dtypes = (F32, BF16, BF16, BF16, BF16, BF16)
    return pl.pallas_call(
        _proj_kernel,
        grid=(m // PROJ_TM,),
        in_specs=[
            pl.BlockSpec((PROJ_TM, D_MODEL), row),
            pl.BlockSpec((1, 1, D_MODEL), per_batch),
            pl.BlockSpec((1, 1, D_MODEL), per_batch),
            _resident((D_MODEL, IN_WIDTH), const),
            _resident((1, IN_WIDTH), const),
            pl.BlockSpec((PROJ_TM, LANES), pos),
            pl.BlockSpec((PROJ_TM, LANES), pos),
            pl.BlockSpec((PROJ_TM, LANES), pos),
        ],
        out_specs=[pl.BlockSpec((PROJ_TM, w), row) for w in out_widths],
        out_shape=[jax.ShapeDtypeStruct((m, w), dt) for w, dt in zip(out_widths, out_dtypes)],
        scratch_shapes=[pltpu.VMEM((PROJ_TM, D_MODEL), BF16)],
        compiler_params=_params("parallel"),
        name="mix_in_proj",
    )(x, shift, scale, w_in, b_in, cosf, sina, sinb)


def _attn_kernel(sinks_ref, q_ref, kp_ref, kc_ref, vp_ref, vc_ref, o_ref):
    has_prev = pl.program_id(1) > 0
    kall = jnp.concatenate([kp_ref[...], kc_ref[...]], axis=0)
    vall = jnp.concatenate([vp_ref[...], vc_ref[...]], axis=0)
    r = lax.broadcasted_iota(jnp.int32, (2 * WINDOW, WINDOW), 0) & (WINDOW - 1)
    c = lax.broadcasted_iota(jnp.int32, (2 * WINDOW, WINDOW), 1)
    own = c <= r
    upper = lax.broadcasted_iota(jnp.int32, (2 * WINDOW, 1), 0) < WINDOW
    lo = lax.broadcasted_iota(jnp.int32, (2 * WINDOW, LANES), 1) < HEAD_DIM
    zero = jnp.zeros((2 * WINDOW, LANES), BF16)
    contract_last = (((1,), (1,)), ((), ()))
    for blk in range(ATTN_TQ // WINDOW):
        rows = slice(blk * WINDOW, (blk + 1) * WINDOW)
        keys = slice(blk * WINDOW, (blk + 2) * WINDOW)
        scores, values = [], []
        for slab in range(KV_WIDTH // LANES):
            lanes = slice(slab * LANES, (slab + 1) * LANES)
            ks, vs = kall[keys, lanes], vall[keys, lanes]
            ks_sw = pltpu.roll(ks, HEAD_DIM, axis=1)
            vs_sw = pltpu.roll(vs, HEAD_DIM, axis=1)
            for half in range(LANES // HEAD_DIM):
                kv = slab * (LANES // HEAD_DIM) + half
                if half == 0:
                    k_lo, k_hi = jnp.where(lo, ks, zero), jnp.where(lo, zero, ks_sw)
                    v_lo, v_hi = jnp.where(lo, vs, zero), jnp.where(lo, zero, vs_sw)
                else:
                    k_lo, k_hi = jnp.where(lo, ks_sw, zero), jnp.where(lo, zero, ks)
                    v_lo, v_hi = jnp.where(lo, vs_sw, zero), jnp.where(lo, zero, vs)
                base = kv * Q_GROUP * HEAD_DIM
                q2 = jnp.concatenate([q_ref[rows, base:base + LANES],
                                      q_ref[rows, base + LANES:base + 2 * LANES]], axis=0)
                for k_x, v_x in ((k_lo, v_lo), (k_hi, v_hi)):
                    scores.append(lax.dot_general(q2, k_x, contract_last,
                                                  preferred_element_type=F32))
                    values.append(v_x)
        probs = []
        for t, s in enumerate(scores):
            head = (t // 2) * Q_GROUP + (t % 2)
            s_prev, s_own = s[:, :WINDOW], s[:, WINDOW:]
            if blk == 0:
                s_prev = jnp.where(has_prev, s_prev, MASK_VALUE)
            dense = jnp.where(own, s_own, s_prev)
            sink = jnp.where(upper, sinks_ref[head], sinks_ref[head + 2])
            m = jnp.maximum(jnp.max(dense, axis=-1, keepdims=True), sink)
            p = jnp.exp(dense - m)
            denom = jnp.sum(p, axis=-1, keepdims=True) + jnp.exp(sink - m)
            p = (p * (1.0 / denom)).astype(BF16)
            pz = jnp.zeros_like(p)
            probs.append(jnp.concatenate([jnp.where(own, pz, p), jnp.where(own, p, pz)], axis=1))
        for kv in range(N_KV_HEADS):
            base = kv * Q_GROUP * HEAD_DIM
            out = (jnp.dot(probs[2 * kv], values[2 * kv], preferred_element_type=F32)
                   + jnp.dot(probs[2 * kv + 1], values[2 * kv + 1], preferred_element_type=F32))
            out = out.astype(BF16)
            o_ref[rows, base:base + LANES] = out[:WINDOW]
            o_ref[rows, base + LANES:base + 2 * LANES] = out[WINDOW:]


def _attention(q, k, v, sinks, *, batch, seq):
    m = q.shape[0]
    nt = seq // ATTN_TQ
    halo_blocks = ATTN_TQ // WINDOW
    cur = lambda b, i: (b * nt + i, 0)
    prev = lambda b, i: (jnp.maximum((b * nt + i) * halo_blocks - 1, 0), 0)
    return pl.pallas_call(
        _attn_kernel,
        grid=(batch, nt),
        in_specs=[
            pl.BlockSpec(memory_space=pltpu.SMEM),
            pl.BlockSpec((ATTN_TQ, ATTN_WIDTH), cur),
            pl.BlockSpec((WINDOW, KV_WIDTH), prev),
            pl.BlockSpec((ATTN_TQ, KV_WIDTH), cur),
            pl.BlockSpec((WINDOW, KV_WIDTH), prev),
            pl.BlockSpec((ATTN_TQ, KV_WIDTH), cur),
        ],
        out_specs=pl.BlockSpec((ATTN_TQ, ATTN_WIDTH), cur),
        out_shape=jax.ShapeDtypeStruct((m, ATTN_WIDTH), BF16),
        compiler_params=_params("parallel", "parallel"),
        name="swa_attention",
    )(sinks, q, k, k, v, v)


def _merge_kernel(xp_ref, xprev_ref, attn_ref, sa_ref, sb_ref, x_ref, gate_ref, g_ref, b_ref,
                  wpool_ref, pscale_ref, wa_ref, wb_ref, wout_ref, o_ref, *, alpha, tiles_per_seq):
    tile_in_seq = pl.program_id(0) % tiles_per_seq
    xp = xp_ref[...]
    halo = jnp.where(tile_in_seq > 0, xprev_ref[...], 0.0)
    xfull = jnp.concatenate([halo, xp], axis=0)
    t1 = tile_in_seq * MERGE_TM + lax.broadcasted_iota(jnp.int32, (MERGE_TM, 1), 0) + 1
    mixed = []
    for gi, w in enumerate(POOL_WINDOWS):
        cols = slice(gi * POOL_GROUP, (gi + 1) * POOL_GROUP)
        s = xfull[:, cols]
        span = 1
        while span < w:
            s = s + pltpu.roll(s, span, axis=0)
            span *= 2
        count = jnp.minimum(t1, w).astype(F32)
        pooled = s[POOL_HALO:] / count - xp[:, cols]
        mg = jnp.dot(pooled.astype(BF16), wpool_ref[gi], preferred_element_type=F32)
        mixed.append((mg * pscale_ref[:, cols]).astype(BF16))
    mixed = jnp.concatenate(mixed, axis=1)
    y_a = jnp.dot(mixed, wa_ref[...], preferred_element_type=F32)
    y_b = jnp.dot(attn_ref[...], wb_ref[...], preferred_element_type=F32)
    merged = (sa_ref[...].astype(F32) * y_a + sb_ref[...].astype(F32) * y_b).astype(BF16)
    y = jnp.dot(merged, wout_ref[...], preferred_element_type=F32)
    z = alpha * x_ref[...] + (1.0 + gate_ref[0]) * y
    o_ref[...] = _layer_norm(z, g_ref[...], b_ref[...])


def _merge_sublayer(xp, attn, sa, sb, x, gate, ln_g, ln_b, w_pool, pool_scale,
                    w_a, w_b, w_out, *, seq, alpha):
    m = x.shape[0]
    tiles_per_seq = seq // MERGE_TM
    halo_blocks = MERGE_TM // POOL_HALO
    row = lambda i: (i, 0)
    prev_rows = lambda i: (jnp.maximum(i * halo_blocks - 1, 0), 0)
    per_batch = lambda i: (i // tiles_per_seq, 0, 0)
    const = lambda i: (0, 0)
    const3 = lambda i: (0, 0, 0)
    return pl.pallas_call(
        functools.partial(_merge_kernel, alpha=alpha, tiles_per_seq=tiles_per_seq),
        grid=(m // MERGE_TM,),
        in_specs=[
            pl.BlockSpec((MERGE_TM, POOL_WIDTH), row),
            pl.BlockSpec((POOL_HALO, POOL_WIDTH), prev_rows),
            pl.BlockSpec((MERGE_TM, ATTN_WIDTH), row),
            pl.BlockSpec((MERGE_TM, GATE_WIDTH), row),
            pl.BlockSpec((MERGE_TM, GATE_WIDTH), row),
            pl.BlockSpec((MERGE_TM, D_MODEL), row),
            pl.BlockSpec((1, 1, D_MODEL), per_batch),
            pl.BlockSpec((1, D_MODEL), const),
            pl.BlockSpec((1, D_MODEL), const),
            _resident((len(POOL_WINDOWS), POOL_GROUP, POOL_GROUP), const3),
            _resident((1, POOL_WIDTH), const),
            _resident((POOL_WIDTH, D_MODEL), const),
            _resident((ATTN_WIDTH, D_MODEL), const),
            _resident((D_MODEL, D_MODEL), const),
        ],
        out_specs=pl.BlockSpec((MERGE_TM, D_MODEL), row),
        out_shape=jax.ShapeDtypeStruct((m, D_MODEL), F32),
        compiler_params=_params("parallel"),
        name="mix_merge",
    )(xp, xp, attn, sa, sb, x, gate, ln_g, ln_b, w_pool, pool_scale, w_a, w_b, w_out)


def _rope_tables(seq):
    pos = jnp.arange(seq, dtype=F32)
    inv_freq = ROPE_THETA ** (-jnp.arange(0, ROT_DIM, 2, dtype=F32) / ROT_DIM)
    ang = pos[:, None] * inv_freq[None, :]
    cos, sin = jnp.cos(ang), jnp.sin(ang)
    d = np.arange(LANES) % HEAD_DIM
    idx = d % ROT_HALF
    lo = jnp.asarray(d < ROT_HALF)
    hi = jnp.asarray((d >= ROT_HALF) & (d < ROT_DIM))
    cosf = jnp.where(lo | hi, cos[:, idx], 1.0)
    sina = jnp.where(lo, -sin[:, idx], 0.0)
    sinb = jnp.where(hi, sin[:, idx], 0.0)
    return cosf, sina, sinb


def kernel(x, c, w_ada, b_ada, ln_g, ln_b, w_ffn1_in, w_ffn1_out, w_in, b_in, w_pool,
           pool_scale, sinks, w_branch_a, w_branch_b, w_out, w_ffn2_in, w_ffn2_out):
    batch, seq, d = x.shape
    depth = w_ada.shape[0]
    assert d == D_MODEL and batch <= MOD_ROWS
    assert seq % FFN_TM == 0 and seq % PROJ_TM == 0 and seq % MERGE_TM == 0 and seq % WINDOW == 0
    alpha = (2 * depth) ** 0.25
    cosf, sina, sinb = _rope_tables(seq)
    c_pad = jnp.pad(c, ((0, MOD_ROWS - batch), (0, 0)))
    h = x.reshape(batch * seq, d)

    for l in range(depth):
        mod = _modulation(c_pad, w_ada[l], b_ada[l])[:batch]
        mod = mod.reshape(batch, N_SUBLAYERS, 3, 1, d)
        shift = lambda s: mod[:, s, 0]
        scale = lambda s: mod[:, s, 1]
        gate = lambda s: mod[:, s, 2]
        g = lambda s: ln_g[l, s].reshape(1, d)
        b = lambda s: ln_b[l, s].reshape(1, d)

        w_gu, w_d = _ffn_weights(w_ffn1_in[l], w_ffn1_out[l])
        h = _ffn_sublayer(h, shift(0), scale(0), gate(0), g(0), b(0), w_gu, w_d,
                          seq=seq, alpha=alpha)

        xp, q, k, v, sa, sb = _input_projection(
            h, shift(1), scale(1), w_in[l].astype(BF16), b_in[l].reshape(1, IN_WIDTH),
            cosf, sina, sinb, seq=seq)
        attn = _attention(q, k, v, sinks[l], batch=batch, seq=seq)
        h = _merge_sublayer(
            xp, attn, sa, sb, h, gate(1), g(1), b(1), w_pool[l].astype(BF16),
            pool_scale[l].reshape(1, POOL_WIDTH), w_branch_a[l].astype(BF16),
            w_branch_b[l].astype(BF16), w_out[l].astype(BF16), seq=seq, alpha=alpha)

        w_gu, w_d = _ffn_weights(w_ffn2_in[l], w_ffn2_out[l])
        h = _ffn_sublayer(h, shift(2), scale(2), gate(2), g(2), b(2), w_gu, w_d,
                          seq=seq, alpha=alpha)
    return h.reshape(batch, seq, d)
```

```python
import functools

import jax
import jax.numpy as jnp
import numpy as np
from jax import lax
from jax.experimental import pallas as pl
from jax.experimental.pallas import tpu as pltpu

F32 = jnp.float32
BF16 = jnp.bfloat16

D_MODEL = 2048
N_Q_HEADS = 16
N_KV_HEADS = 4
HEAD_DIM = 64
Q_GROUP = N_Q_HEADS // N_KV_HEADS
ATTN_WIDTH = N_Q_HEADS * HEAD_DIM
KV_WIDTH = N_KV_HEADS * HEAD_DIM
WINDOW = 128
ROPE_THETA = 500000.0
ROT_DIM = HEAD_DIM // 4
ROT_HALF = ROT_DIM // 2
POOL_WINDOWS = (2, 4, 8, 16)
POOL_WIDTH = D_MODEL // 2
POOL_GROUP = POOL_WIDTH // len(POOL_WINDOWS)
POOL_HALO = 16
GATE_WIDTH = D_MODEL
IN_WIDTH = POOL_WIDTH + ATTN_WIDTH + 2 * KV_WIDTH + 2 * GATE_WIDTH
D_FF = 5504
N_SUBLAYERS = 3
LN_EPS = 1e-5
MASK_VALUE = -1e30

LANES = 128
VMEM_LIMIT = 60 * 1024 * 1024
FFN_TM = 1024
FFN_MC = 512
FFN_EDGE_MC = 256
FFN_TF = 512
D_FF_PAD = ((D_FF + FFN_TF - 1) // FFN_TF) * FFN_TF
CAST_ROWS_IN = 64
CAST_ROWS_OUT = 128
CAST_OUT_PARTS = 4
assert D_FF % LANES == 0 and D_FF % CAST_ROWS_OUT == 0
assert D_FF_PAD % (CAST_ROWS_OUT * CAST_OUT_PARTS) == 0
PROJ_TM = 512
PROJ_TN = 512
ATTN_TQ = 512
MERGE_TM = 512
MERGE_MC = 256
ROW_CHUNK = 128
MOD_TN = 1024
MOD_ROWS = 8


def _params(*sem):
    return pltpu.CompilerParams(dimension_semantics=sem, vmem_limit_bytes=VMEM_LIMIT)


def _resident(shape, index_map):
    return pl.BlockSpec(shape, index_map, pipeline_mode=pl.Buffered(1))


def _layer_norm(z, g, b):
    mu = jnp.mean(z, axis=-1, keepdims=True)
    zc = z - mu
    var = jnp.mean(zc * zc, axis=-1, keepdims=True)
    return zc * lax.rsqrt(var + LN_EPS) * g + b


def _mod_kernel(c_ref, w_ref, b_ref, o_ref):
    c = c_ref[...]
    c_act = (c * jax.nn.sigmoid(c)).astype(BF16)
    o_ref[...] = jnp.dot(c_act, w_ref[...].astype(BF16),
                         preferred_element_type=F32) + b_ref[...]


def _modulation(c_pad, w_ada, b_ada):
    n = w_ada.shape[1]
    return pl.pallas_call(
        _mod_kernel,
        grid=(n // MOD_TN,),
        in_specs=[
            pl.BlockSpec((MOD_ROWS, D_MODEL), lambda j: (0, 0)),
            pl.BlockSpec((D_MODEL, MOD_TN), lambda j: (0, j)),
            pl.BlockSpec((1, MOD_TN), lambda j: (0, j)),
        ],
        out_specs=pl.BlockSpec((MOD_ROWS, MOD_TN), lambda j: (0, j)),
        out_shape=jax.ShapeDtypeStruct((MOD_ROWS, n), F32),
        compiler_params=_params("arbitrary"),
        name="adaln_mod",
    )(c_pad, w_ada, b_ada.reshape(1, n))


def _cast_in_kernel(w_ref, o_ref):
    o_ref[0, :, :D_FF] = w_ref[:, :D_FF].astype(BF16)
    o_ref[1, :, :D_FF] = w_ref[:, D_FF:].astype(BF16)
    o_ref[:, :, D_FF:] = jnp.zeros((2, CAST_ROWS_IN, D_FF_PAD - D_FF), BF16)


def _cast_out_kernel(*refs):
    w_refs, o_ref = refs[:-1], refs[-1]
    j = pl.program_id(0)
    for t, w_ref in enumerate(w_refs):
        rows = slice(t * CAST_ROWS_OUT, (t + 1) * CAST_ROWS_OUT)
        in_range = j * CAST_OUT_PARTS + t < D_FF // CAST_ROWS_OUT

        @pl.when(in_range)
        def _():
            o_ref[rows, :] = w_ref[...].astype(BF16)

        @pl.when(jnp.logical_not(in_range))
        def _():
            o_ref[rows, :] = jnp.zeros((CAST_ROWS_OUT, D_MODEL), BF16)


def _ffn_weights(w_in, w_out):
    w_gu = pl.pallas_call(
        _cast_in_kernel,
        grid=(D_MODEL // CAST_ROWS_IN,),
        in_specs=[pl.BlockSpec((CAST_ROWS_IN, 2 * D_FF), lambda i: (i, 0))],
        out_specs=pl.BlockSpec((2, CAST_ROWS_IN, D_FF_PAD), lambda i: (0, i, 0)),
        out_shape=jax.ShapeDtypeStruct((2, D_MODEL, D_FF_PAD), BF16),
        compiler_params=_params("parallel"),
        name="cast_ffn_in",
    )(w_in)
    last_in = D_FF // CAST_ROWS_OUT - 1
    part = lambda t: pl.BlockSpec(
        (CAST_ROWS_OUT, D_MODEL), lambda j: (jnp.minimum(j * CAST_OUT_PARTS + t, last_in), 0))
    w_d = pl.pallas_call(
        _cast_out_kernel,
        grid=(D_FF_PAD // (CAST_ROWS_OUT * CAST_OUT_PARTS),),
        in_specs=[part(t) for t in range(CAST_OUT_PARTS)],
        out_specs=pl.BlockSpec((CAST_ROWS_OUT * CAST_OUT_PARTS, D_MODEL), lambda j: (j, 0)),
        out_shape=jax.ShapeDtypeStruct((D_FF_PAD, D_MODEL), BF16),
        compiler_params=_params("parallel"),
        name="cast_ffn_out",
    )(*([w_out] * CAST_OUT_PARTS))
    return w_gu, w_d


def _swiglu_chain(u, wg_ref, wu_ref, wd_ref):
    a = jnp.dot(u, wg_ref[...], preferred_element_type=F32)
    b = jnp.dot(u, wu_ref[...], preferred_element_type=F32)
    h = (a * jax.nn.sigmoid(a) * b).astype(BF16)
    return jnp.dot(h, wd_ref[...], preferred_element_type=F32)


def _ffn_kernel(x_ref, shift_ref, scale_ref, gate_ref, g_ref, b_ref, wg_ref, wu_ref, wd_ref,
                o_ref, u_scr, *, alpha):
    j = pl.program_id(1)
    last = pl.num_programs(1) - 1

    @pl.when(j == 0)
    def _():
        scale1 = 1.0 + scale_ref[0]
        shift = shift_ref[0]
        for r in range(FFN_TM // FFN_EDGE_MC):
            rows = slice(r * FFN_EDGE_MC, (r + 1) * FFN_EDGE_MC)
            for s in range(r * FFN_EDGE_MC, (r + 1) * FFN_EDGE_MC, ROW_CHUNK):
                sub = slice(s, s + ROW_CHUNK)
                u_scr[sub, :] = (x_ref[sub, :] * scale1 + shift).astype(BF16)
            o_ref[rows, :] = _swiglu_chain(u_scr[rows, :], wg_ref, wu_ref, wd_ref)

    @pl.when((j > 0) & (j < last))
    def _():
        for r in range(FFN_TM // FFN_MC):
            rows = slice(r * FFN_MC, (r + 1) * FFN_MC)
            o_ref[rows, :] += _swiglu_chain(u_scr[rows, :], wg_ref, wu_ref, wd_ref)

    @pl.when(j == last)
    def _():
        gate_half = 0.5 * (1.0 + gate_ref[0])
        g = g_ref[...]
        beta = b_ref[...]

        def finish(r):
            for s in range(r * FFN_EDGE_MC, (r + 1) * FFN_EDGE_MC, ROW_CHUNK):
                sub = slice(s, s + ROW_CHUNK)
                z = alpha * x_ref[sub, :] + gate_half * o_ref[sub, :]
                o_ref[sub, :] = _layer_norm(z, g, beta)

        n_chains = FFN_TM // FFN_EDGE_MC
        for r in range(n_chains):
            rows = slice(r * FFN_EDGE_MC, (r + 1) * FFN_EDGE_MC)
            o_ref[rows, :] += _swiglu_chain(u_scr[rows, :], wg_ref, wu_ref, wd_ref)
            if r > 0:
                finish(r - 1)
        finish(n_chains - 1)


def _ffn_sublayer(x, shift, scale, gate, ln_g, ln_b, w_gu, w_d, *, seq, alpha):
    m = x.shape[0]
    tiles_per_seq = seq // FFN_TM
    row = lambda i, j: (i, 0)
    per_batch = lambda i, j: (i // tiles_per_seq, 0, 0)
    const = lambda i, j: (0, 0)
    return pl.pallas_call(
        functools.partial(_ffn_kernel, alpha=alpha),
        grid=(m // FFN_TM, D_FF_PAD // FFN_TF),
        in_specs=[
            pl.BlockSpec((FFN_TM, D_MODEL), row),
            pl.BlockSpec((1, 1, D_MODEL), per_batch),
            pl.BlockSpec((1, 1, D_MODEL), per_batch),
            pl.BlockSpec((1, 1, D_MODEL), per_batch),
            pl.BlockSpec((1, D_MODEL), const),
            pl.BlockSpec((1, D_MODEL), const),
            pl.BlockSpec((None, D_MODEL, FFN_TF), lambda i, j: (0, 0, j)),
            pl.BlockSpec((None, D_MODEL, FFN_TF), lambda i, j: (1, 0, j)),
            pl.BlockSpec((FFN_TF, D_MODEL), lambda i, j: (j, 0)),
        ],
        out_specs=pl.BlockSpec((FFN_TM, D_MODEL), row),
        out_shape=jax.ShapeDtypeStruct((m, D_MODEL), F32),
        scratch_shapes=[pltpu.VMEM((FFN_TM, D_MODEL), BF16)],
        compiler_params=_params("parallel", "arbitrary"),
        name="ffn_sublayer",
    )(x, shift, scale, gate, ln_g, ln_b, w_gu, w_gu, w_d)


def _rope(t, cosf, sina, sinb):
    up = pltpu.roll(t, LANES - ROT_HALF, axis=1)
    dn = pltpu.roll(t, ROT_HALF, axis=1)
    return t * cosf + up * sina + dn * sinb


def _rope_wide(t, cosf, sina, sinb):
    slabs = [_rope(t[:, s:s + LANES], cosf, sina, sinb)
             for s in range(0, t.shape[1], LANES)]
    return jnp.concatenate(slabs, axis=1)


def _proj_kernel(x_ref, shift_ref, scale_ref, w_ref, b_ref, cos_ref, sina_ref, sinb_ref,
                 xp_ref, q_ref, k_ref, v_ref, sa_ref, sb_ref, u_scr):
    scale1 = 1.0 + scale_ref[0]
    shift = shift_ref[0]

    def body(r, carry):
        rows = pl.ds(pl.multiple_of(r * ROW_CHUNK, ROW_CHUNK), ROW_CHUNK)
        u_scr[rows, :] = (x_ref[rows, :] * scale1 + shift).astype(BF16)
        return carry

    lax.fori_loop(0, PROJ_TM // ROW_CHUNK, body, 0)

    u = u_scr[...]
    cosf = cos_ref[...]
    sina = sina_ref[...]
    sinb = sinb_ref[...]
    q_off = POOL_WIDTH
    kv_off = q_off + ATTN_WIDTH
    ga_off = kv_off + 2 * KV_WIDTH
    gb_off = ga_off + GATE_WIDTH
    for col in range(0, IN_WIDTH, PROJ_TN):
        cols = slice(col, col + PROJ_TN)
        h = jnp.dot(u, w_ref[:, cols], preferred_element_type=F32) + b_ref[:, cols]
        if col < q_off:
            xp_ref[:, cols] = h
        elif col < kv_off:
            q = _rope_wide(h, cosf, sina, sinb) * (HEAD_DIM ** -0.5)
            q_ref[:, col - q_off:col - q_off + PROJ_TN] = q.astype(BF16)
        elif col < ga_off:
            k_ref[...] = _rope_wide(h[:, :KV_WIDTH], cosf, sina, sinb).astype(BF16)
            v_ref[...] = h[:, KV_WIDTH:].astype(BF16)
        elif col < gb_off:
            sa_ref[:, col - ga_off:col - ga_off + PROJ_TN] = jax.nn.sigmoid(h).astype(BF16)
        else:
            sb_ref[:, col - gb_off:col - gb_off + PROJ_TN] = jax.nn.sigmoid(h).astype(BF16)


def _input_projection(x, shift, scale, w_in, b_in, cosf, sina, sinb, *, seq):
    m = x.shape[0]
    tiles_per_seq = seq // PROJ_TM
    row = lambda i: (i, 0)
    per_batch = lambda i: (i // tiles_per_seq, 0, 0)
    const = lambda i: (0, 0)
    pos = lambda i: (i % tiles_per_seq, 0)
    out_widths = (POOL_WIDTH, ATTN_WIDTH, KV_WIDTH, KV_WIDTH, GATE_WIDTH, GATE_WIDTH)
    out_dtypes = (F32, BF16, BF16, BF16, BF16, BF16)
    return pl.pallas_call(
        _proj_kernel,
        grid=(m // PROJ_TM,),
        in_specs=[
            pl.BlockSpec((PROJ_TM, D_MODEL), row),
            pl.BlockSpec((1, 1, D_MODEL), per_batch),
            pl.BlockSpec((1, 1, D_MODEL), per_batch),
            _resident((D_MODEL, IN_WIDTH), const),
            _resident((1, IN_WIDTH), const),
            pl.BlockSpec((PROJ_TM, LANES), pos),
            pl.BlockSpec((PROJ_TM, LANES), pos),
            pl.BlockSpec((PROJ_TM, LANES), pos),
        ],
        out_specs=[pl.BlockSpec((PROJ_TM, w), row) for w in out_widths],
        out_shape=[jax.ShapeDtypeStruct((m, w), dt) for w, dt in zip(out_widths, out_dtypes)],
        scratch_shapes=[pltpu.VMEM((PROJ_TM, D_MODEL), BF16)],
        compiler_params=_params("parallel"),
        name="mix_in_proj",
    )(x, shift, scale, w_in, b_in, cosf, sina, sinb)


def _attn_kernel(sinks_ref, q_ref, kp_ref, kc_ref, vp_ref, vc_ref, o_ref):
    has_prev = pl.program_id(1) > 0
    kall = jnp.concatenate([kp_ref[...], kc_ref[...]], axis=0)
    vall = jnp.concatenate([vp_ref[...], vc_ref[...]], axis=0)
    r = lax.broadcasted_iota(jnp.int32, (2 * WINDOW, WINDOW), 0) & (WINDOW - 1)
    c = lax.broadcasted_iota(jnp.int32, (2 * WINDOW, WINDOW), 1)
    own = c <= r
    upper = lax.broadcasted_iota(jnp.int32, (2 * WINDOW, 1), 0) < WINDOW
    lo = lax.broadcasted_iota(jnp.int32, (2 * WINDOW, LANES), 1) < HEAD_DIM
    zero = jnp.zeros((2 * WINDOW, LANES), BF16)
    contract_last = (((1,), (1,)), ((), ()))
    for blk in range(ATTN_TQ // WINDOW):
        rows = slice(blk * WINDOW, (blk + 1) * WINDOW)
        keys = slice(blk * WINDOW, (blk + 2) * WINDOW)
        scores, values = [], []
        for slab in range(KV_WIDTH // LANES):
            lanes = slice(slab * LANES, (slab + 1) * LANES)
            ks, vs = kall[keys, lanes], vall[keys, lanes]
            ks_sw = pltpu.roll(ks, HEAD_DIM, axis=1)
            vs_sw = pltpu.roll(vs, HEAD_DIM, axis=1)
            for half in range(LANES // HEAD_DIM):
                kv = slab * (LANES // HEAD_DIM) + half
                if half == 0:
                    k_lo, k_hi = jnp.where(lo, ks, zero), jnp.where(lo, zero, ks_sw)
                    v_lo, v_hi = jnp.where(lo, vs, zero), jnp.where(lo, zero, vs_sw)
                else:
                    k_lo, k_hi = jnp.where(lo, ks_sw, zero), jnp.where(lo, zero, ks)
                    v_lo, v_hi = jnp.where(lo, vs_sw, zero), jnp.where(lo, zero, vs)
                base = kv * Q_GROUP * HEAD_DIM
                q2 = jnp.concatenate([q_ref[rows, base:base + LANES],
                                      q_ref[rows, base + LANES:base + 2 * LANES]], axis=0)
                for k_x, v_x in ((k_lo, v_lo), (k_hi, v_hi)):
                    scores.append(lax.dot_general(q2, k_x, contract_last,
                                                  preferred_element_type=F32))
                    values.append(v_x)
        probs = []
        for t, s in enumerate(scores):
            head = (t // 2) * Q_GROUP + (t % 2)
            s_prev, s_own = s[:, :WINDOW], s[:, WINDOW:]
            if blk == 0:
                s_prev = jnp.where(has_prev, s_prev, MASK_VALUE)
            dense = jnp.where(own, s_own, s_prev)
            sink = jnp.where(upper, sinks_ref[head], sinks_ref[head + 2])
            m = jnp.maximum(jnp.max(dense, axis=-1, keepdims=True), sink)
            p = jnp.exp(dense - m)
            denom = jnp.sum(p, axis=-1, keepdims=True) + jnp.exp(sink - m)
            p = (p * (1.0 / denom)).astype(BF16)
            pz = jnp.zeros_like(p)
            probs.append(jnp.concatenate([jnp.where(own, pz, p), jnp.where(own, p, pz)], axis=1))
        for kv in range(N_KV_HEADS):
            base = kv * Q_GROUP * HEAD_DIM
            out = (jnp.dot(probs[2 * kv], values[2 * kv], preferred_element_type=F32)
                   + jnp.dot(probs[2 * kv + 1], values[2 * kv + 1], preferred_element_type=F32))
            out = out.astype(BF16)
            o_ref[rows, base:base + LANES] = out[:WINDOW]
            o_ref[rows, base + LANES:base + 2 * LANES] = out[WINDOW:]


def _attention(q, k, v, sinks, *, batch, seq):
    m = q.shape[0]
    nt = seq // ATTN_TQ
    halo_blocks = ATTN_TQ // WINDOW
    cur = lambda b, i: (b * nt + i, 0)
    prev = lambda b, i: (jnp.maximum((b * nt + i) * halo_blocks - 1, 0), 0)
    return pl.pallas_call(
        _attn_kernel,
        grid=(batch, nt),
        in_specs=[
            pl.BlockSpec(memory_space=pltpu.SMEM),
            pl.BlockSpec((ATTN_TQ, ATTN_WIDTH), cur),
            pl.BlockSpec((WINDOW, KV_WIDTH), prev),
            pl.BlockSpec((ATTN_TQ, KV_WIDTH), cur),
            pl.BlockSpec((WINDOW, KV_WIDTH), prev),
            pl.BlockSpec((ATTN_TQ, KV_WIDTH), cur),
        ],
        out_specs=pl.BlockSpec((ATTN_TQ, ATTN_WIDTH), cur),
        out_shape=jax.ShapeDtypeStruct((m, ATTN_WIDTH), BF16),
        compiler_params=_params("parallel", "parallel"),
        name="swa_attention",
    )(sinks, q, k, k, v, v)


def _merge_kernel(xp_ref, xprev_ref, attn_ref, sa_ref, sb_ref, x_ref, gate_ref, g_ref, b_ref,
                  wpool_ref, pscale_ref, wa_ref, wb_ref, wout_ref, o_ref, *, alpha, tiles_per_seq):
    tile_in_seq = pl.program_id(0) % tiles_per_seq
    gate1 = 1.0 + gate_ref[0]
    for start in range(0, MERGE_TM, MERGE_MC):
        rows = slice(start, start + MERGE_MC)
        xp = xp_ref[rows, :]
        if start == 0:
            halo = jnp.where(tile_in_seq > 0, xprev_ref[...], 0.0)
        else:
            halo = xp_ref[start - POOL_HALO:start, :]
        xfull = jnp.concatenate([halo, xp], axis=0)
        t1 = (tile_in_seq * MERGE_TM + start + 1
              + lax.broadcasted_iota(jnp.int32, (MERGE_MC, 1), 0))
        mixed = []
        for gi, w in enumerate(POOL_WINDOWS):
            cols = slice(gi * POOL_GROUP, (gi + 1) * POOL_GROUP)
            s = xfull[:, cols]
            span = 1
            while span < w:
                s = s + pltpu.roll(s, span, axis=0)
                span *= 2
            count = jnp.minimum(t1, w).astype(F32)
            pooled = s[POOL_HALO:] / count - xp[:, cols]
            mg = jnp.dot(pooled.astype(BF16), wpool_ref[gi], preferred_element_type=F32)
            mixed.append((mg * pscale_ref[:, cols]).astype(BF16))
        mixed = jnp.concatenate(mixed, axis=1)
        y_a = jnp.dot(mixed, wa_ref[...], preferred_element_type=F32)
        y_b = jnp.dot(attn_ref[rows, :], wb_ref[...], preferred_element_type=F32)
        merged = (sa_ref[rows, :].astype(F32) * y_a
                  + sb_ref[rows, :].astype(F32) * y_b).astype(BF16)
        y = jnp.dot(merged, wout_ref[...], preferred_element_type=F32)
        z = alpha * x_ref[rows, :] + gate1 * y
        o_ref[rows, :] = _layer_norm(z, g_ref[...], b_ref[...])


def _merge_sublayer(xp, attn, sa, sb, x, gate, ln_g, ln_b, w_pool, pool_scale,
                    w_a, w_b, w_out, *, seq, alpha):
    m = x.shape[0]
    tiles_per_seq = seq // MERGE_TM
    halo_blocks = MERGE_TM // POOL_HALO
    row = lambda i: (i, 0)
    prev_rows = lambda i: (jnp.maximum(i * halo_blocks - 1, 0), 0)
    per_batch = lambda i: (i // tiles_per_seq, 0, 0)
    const = lambda i: (0, 0)
    const3 = lambda i: (0, 0, 0)
    return pl.pallas_call(
        functools.partial(_merge_kernel, alpha=alpha, tiles_per_seq=tiles_per_seq),
        grid=(m // MERGE_TM,),
        in_specs=[
            pl.BlockSpec((MERGE_TM, POOL_WIDTH), row),
            pl.BlockSpec((POOL_HALO, POOL_WIDTH), prev_rows),
            pl.BlockSpec((MERGE_TM, ATTN_WIDTH), row),
            pl.BlockSpec((MERGE_TM, GATE_WIDTH), row),
            pl.BlockSpec((MERGE_TM, GATE_WIDTH), row),
            pl.BlockSpec((MERGE_TM, D_MODEL), row),
            pl.BlockSpec((1, 1, D_MODEL), per_batch),
            pl.BlockSpec((1, D_MODEL), const),
            pl.BlockSpec((1, D_MODEL), const),
            _resident((len(POOL_WINDOWS), POOL_GROUP, POOL_GROUP), const3),
            _resident((1, POOL_WIDTH), const),
            _resident((POOL_WIDTH, D_MODEL), const),
            _resident((ATTN_WIDTH, D_MODEL), const),
            _resident((D_MODEL, D_MODEL), const),
        ],
        out_specs=pl.BlockSpec((MERGE_TM, D_MODEL), row),
        out_shape=jax.ShapeDtypeStruct((m, D_MODEL), F32),
        compiler_params=_params("parallel"),
        name="mix_merge",
    )(xp, xp, attn, sa, sb, x, gate, ln_g, ln_b, w_pool, pool_scale, w_a, w_b, w_out)


def _rope_tables(seq):
    pos = jnp.arange(seq, dtype=F32)
    inv_freq = ROPE_THETA ** (-jnp.arange(0, ROT_DIM, 2, dtype=F32) / ROT_DIM)
    ang = pos[:, None] * inv_freq[None, :]
    cos, sin = jnp.cos(ang), jnp.sin(ang)
    d = np.arange(LANES) % HEAD_DIM
    idx = d % ROT_HALF
    lo = jnp.asarray(d < ROT_HALF)
    hi = jnp.asarray((d >= ROT_HALF) & (d < ROT_DIM))
    cosf = jnp.where(lo | hi, cos[:, idx], 1.0)
    sina = jnp.where(lo, -sin[:, idx], 0.0)
    sinb = jnp.where(hi, sin[:, idx], 0.0)
    return cosf, sina, sinb


def kernel(x, c, w_ada, b_ada, ln_g, ln_b, w_ffn1_in, w_ffn1_out, w_in, b_in, w_pool,
           pool_scale, sinks, w_branch_a, w_branch_b, w_out, w_ffn2_in, w_ffn2_out):
    batch, seq, d = x.shape
    depth = w_ada.shape[0]
    assert d == D_MODEL and batch <= MOD_ROWS
    assert seq % FFN_TM == 0 and seq % PROJ_TM == 0 and seq % MERGE_TM == 0 and seq % WINDOW == 0
    alpha = (2 * depth) ** 0.25
    cosf, sina, sinb = _rope_tables(seq)
    c_pad = jnp.pad(c, ((0, MOD_ROWS - batch), (0, 0)))
    h = x.reshape(batch * seq, d)

    for l in range(depth):
        mod = _modulation(c_pad, w_ada[l], b_ada[l])[:batch]
        mod = mod.reshape(batch, N_SUBLAYERS, 3, 1, d)
        shift = lambda s: mod[:, s, 0]
        scale = lambda s: mod[:, s, 1]
        gate = lambda s: mod[:, s, 2]
        g = lambda s: ln_g[l, s].reshape(1, d)
        b = lambda s: ln_b[l, s].reshape(1, d)

        w_gu, w_d = _ffn_weights(w_ffn1_in[l], w_ffn1_out[l])
        h = _ffn_sublayer(h, shift(0), scale(0), gate(0), g(0), b(0), w_gu, w_d,
                          seq=seq, alpha=alpha)

        xp, q, k, v, sa, sb = _input_projection(
            h, shift(1), scale(1), w_in[l].astype(BF16), b_in[l].reshape(1, IN_WIDTH),
            cosf, sina, sinb, seq=seq)
        attn = _attention(q, k, v, sinks[l], batch=batch, seq=seq)
        h = _merge_sublayer(
            xp, attn, sa, sb, h, gate(1), g(1), b(1), w_pool[l].astype(BF16),
            pool_scale[l].reshape(1, POOL_WIDTH), w_branch_a[l].astype(BF16),
            w_branch_b[l].astype(BF16), w_out[l].astype(BF16), seq=seq, alpha=alpha)

        w_gu, w_d = _ffn_weights(w_ffn2_in[l], w_ffn2_out[l])
        h = _ffn_sublayer(h, shift(2), scale(2), gate(2), g(2), b(2), w_gu, w_d,
                          seq=seq, alpha=alpha)
    return h.reshape(batch, seq, d)
```

```python
import functools

import jax
import jax.numpy as jnp
import numpy as np
from jax import lax
from jax.experimental import pallas as pl
from jax.experimental.pallas import tpu as pltpu

F32 = jnp.float32
BF16 = jnp.bfloat16

D_MODEL = 2048
N_Q_HEADS = 16
N_KV_HEADS = 4
HEAD_DIM = 64
Q_GROUP = N_Q_HEADS // N_KV_HEADS
ATTN_WIDTH = N_Q_HEADS * HEAD_DIM
KV_WIDTH = N_KV_HEADS * HEAD_DIM
WINDOW = 128
ROPE_THETA = 500000.0
ROT_DIM = HEAD_DIM // 4
ROT_HALF = ROT_DIM // 2
POOL_WINDOWS = (2, 4, 8, 16)
POOL_WIDTH = D_MODEL // 2
POOL_GROUP = POOL_WIDTH // len(POOL_WINDOWS)
POOL_HALO = 16
GATE_WIDTH = D_MODEL
IN_WIDTH = POOL_WIDTH + ATTN_WIDTH + 2 * KV_WIDTH + 2 * GATE_WIDTH
D_FF = 5504
N_SUBLAYERS = 3
LN_EPS = 1e-5
MASK_VALUE = -1e30

LANES = 128
VMEM_LIMIT = 60 * 1024 * 1024
FFN_TM = 1024
FFN_MC = 512
FFN_EDGE_MC = 256
FFN_TF = 512
D_FF_PAD = ((D_FF + FFN_TF - 1) // FFN_TF) * FFN_TF
CAST_ROWS_IN = 128
CAST_ROWS_OUT = 128
CAST_OUT_PARTS = 4
SIDE_ROWS_IN = 16
SIDE_ROWS_OUT = 32
assert D_FF % LANES == 0 and D_FF % CAST_ROWS_OUT == 0
assert D_FF_PAD % (CAST_ROWS_OUT * CAST_OUT_PARTS) == 0
PROJ_TM = 512
PROJ_MC = 256
PROJ_TN = 512
ATTN_TQ = 512
MERGE_TM = 512
MERGE_MC = 256
ROW_CHUNK = 128
MOD_TN = 1024
MOD_ROWS = 8


def _params(*sem):
    return pltpu.CompilerParams(dimension_semantics=sem, vmem_limit_bytes=VMEM_LIMIT)


def _resident(shape, index_map):
    return pl.BlockSpec(shape, index_map, pipeline_mode=pl.Buffered(1))


def _layer_norm(z, g, b):
    mu = jnp.mean(z, axis=-1, keepdims=True)
    zc = z - mu
    var = jnp.mean(zc * zc, axis=-1, keepdims=True)
    return zc * lax.rsqrt(var + LN_EPS) * g + b


def _mod_kernel(c_ref, w_ref, b_ref, o_ref):
    c = c_ref[...]
    c_act = (c * jax.nn.sigmoid(c)).astype(BF16)
    o_ref[...] = jnp.dot(c_act, w_ref[...].astype(BF16),
                         preferred_element_type=F32) + b_ref[...]


def _modulation(c_pad, w_ada, b_ada):
    n = w_ada.shape[1]
    return pl.pallas_call(
        _mod_kernel,
        grid=(n // MOD_TN,),
        in_specs=[
            pl.BlockSpec((MOD_ROWS, D_MODEL), lambda j: (0, 0)),
            pl.BlockSpec((D_MODEL, MOD_TN), lambda j: (0, j)),
            pl.BlockSpec((1, MOD_TN), lambda j: (0, j)),
        ],
        out_specs=pl.BlockSpec((MOD_ROWS, MOD_TN), lambda j: (0, j)),
        out_shape=jax.ShapeDtypeStruct((MOD_ROWS, n), F32),
        compiler_params=_params("arbitrary"),
        name="adaln_mod",
    )(c_pad, w_ada, b_ada.reshape(1, n))


def _cast_in_kernel(w_ref, o_ref):
    o_ref[0, :, :D_FF] = w_ref[:, :D_FF].astype(BF16)
    o_ref[1, :, :D_FF] = w_ref[:, D_FF:].astype(BF16)
    o_ref[:, :, D_FF:] = jnp.zeros((2, o_ref.shape[1], D_FF_PAD - D_FF), BF16)


def _cast_out_kernel(*refs):
    w_refs, o_ref = refs[:-1], refs[-1]
    j = pl.program_id(0)
    for t, w_ref in enumerate(w_refs):
        rows = slice(t * CAST_ROWS_OUT, (t + 1) * CAST_ROWS_OUT)
        in_range = j * CAST_OUT_PARTS + t < D_FF // CAST_ROWS_OUT

        @pl.when(in_range)
        def _():
            o_ref[rows, :] = w_ref[...].astype(BF16)

        @pl.when(jnp.logical_not(in_range))
        def _():
            o_ref[rows, :] = jnp.zeros((CAST_ROWS_OUT, D_MODEL), BF16)


def _ffn_weights(w_in, w_out):
    w_gu = pl.pallas_call(
        _cast_in_kernel,
        grid=(D_MODEL // CAST_ROWS_IN,),
        in_specs=[pl.BlockSpec((CAST_ROWS_IN, 2 * D_FF), lambda i: (i, 0))],
        out_specs=pl.BlockSpec((2, CAST_ROWS_IN, D_FF_PAD), lambda i: (0, i, 0)),
        out_shape=jax.ShapeDtypeStruct((2, D_MODEL, D_FF_PAD), BF16),
        compiler_params=_params("parallel"),
        name="cast_ffn_in",
    )(w_in)
    last_in = D_FF // CAST_ROWS_OUT - 1
    part = lambda t: pl.BlockSpec(
        (CAST_ROWS_OUT, D_MODEL), lambda j: (jnp.minimum(j * CAST_OUT_PARTS + t, last_in), 0))
    w_d = pl.pallas_call(
        _cast_out_kernel,
        grid=(D_FF_PAD // (CAST_ROWS_OUT * CAST_OUT_PARTS),),
        in_specs=[part(t) for t in range(CAST_OUT_PARTS)],
        out_specs=pl.BlockSpec((CAST_ROWS_OUT * CAST_OUT_PARTS, D_MODEL), lambda j: (j, 0)),
        out_shape=jax.ShapeDtypeStruct((D_FF_PAD, D_MODEL), BF16),
        compiler_params=_params("parallel"),
        name="cast_ffn_out",
    )(*([w_out] * CAST_OUT_PARTS))
    return w_gu, w_d


def _swiglu_chain(u, wg_ref, wu_ref, wd_ref):
    a = jnp.dot(u, wg_ref[...], preferred_element_type=F32)
    b = jnp.dot(u, wu_ref[...], preferred_element_type=F32)
    h = (a * jax.nn.sigmoid(a) * b).astype(BF16)
    return jnp.dot(h, wd_ref[...], preferred_element_type=F32)


def _ffn_kernel(x_ref, shift_ref, scale_ref, gate_ref, g_ref, b_ref, wg_ref, wu_ref, wd_ref,
                *rest, alpha, side_cast):
    j = pl.program_id(1)
    last = pl.num_programs(1) - 1
    if side_cast:
        next_in_ref, next_out_ref, o_ref, cast_in_ref, cast_out_ref, u_scr = rest
        step = pl.program_id(0) * pl.num_programs(1) + j

        def side_work():
            _cast_in_kernel(next_in_ref, cast_in_ref)
            cast_out_ref[...] = jnp.where(step < D_FF // SIDE_ROWS_OUT,
                                          next_out_ref[...], 0.0).astype(BF16)
    else:
        o_ref, u_scr = rest

        def side_work():
            pass

    @pl.when(j == 0)
    def _():
        scale1 = 1.0 + scale_ref[0]
        shift = shift_ref[0]
        for r in range(FFN_TM // FFN_EDGE_MC):
            rows = slice(r * FFN_EDGE_MC, (r + 1) * FFN_EDGE_MC)
            for s in range(r * FFN_EDGE_MC, (r + 1) * FFN_EDGE_MC, ROW_CHUNK):
                sub = slice(s, s + ROW_CHUNK)
                u_scr[sub, :] = (x_ref[sub, :] * scale1 + shift).astype(BF16)
            o_ref[rows, :] = _swiglu_chain(u_scr[rows, :], wg_ref, wu_ref, wd_ref)
        side_work()

    @pl.when((j > 0) & (j < last))
    def _():
        for r in range(FFN_TM // FFN_MC):
            rows = slice(r * FFN_MC, (r + 1) * FFN_MC)
            o_ref[rows, :] += _swiglu_chain(u_scr[rows, :], wg_ref, wu_ref, wd_ref)
        side_work()

    @pl.when(j == last)
    def _():
        gate_half = 0.5 * (1.0 + gate_ref[0])
        g = g_ref[...]
        beta = b_ref[...]

        def finish(r):
            for s in range(r * FFN_EDGE_MC, (r + 1) * FFN_EDGE_MC, ROW_CHUNK):
                sub = slice(s, s + ROW_CHUNK)
                z = alpha * x_ref[sub, :] + gate_half * o_ref[sub, :]
                o_ref[sub, :] = _layer_norm(z, g, beta)

        n_chains = FFN_TM // FFN_EDGE_MC
        for r in range(n_chains):
            rows = slice(r * FFN_EDGE_MC, (r + 1) * FFN_EDGE_MC)
            o_ref[rows, :] += _swiglu_chain(u_scr[rows, :], wg_ref, wu_ref, wd_ref)
            if r > 0:
                finish(r - 1)
        finish(n_chains - 1)
        side_work()


def _ffn_sublayer(x, shift, scale, gate, ln_g, ln_b, w_gu, w_d, *, seq, alpha, next_weights=None):
    m = x.shape[0]
    tiles_per_seq = seq // FFN_TM
    n_tiles, n_chunks = m // FFN_TM, D_FF_PAD // FFN_TF
    row = lambda i, j: (i, 0)
    per_batch = lambda i, j: (i // tiles_per_seq, 0, 0)
    const = lambda i, j: (0, 0)
    in_specs = [
        pl.BlockSpec((FFN_TM, D_MODEL), row),
        pl.BlockSpec((1, 1, D_MODEL), per_batch),
        pl.BlockSpec((1, 1, D_MODEL), per_batch),
        pl.BlockSpec((1, 1, D_MODEL), per_batch),
        pl.BlockSpec((1, D_MODEL), const),
        pl.BlockSpec((1, D_MODEL), const),
        pl.BlockSpec((None, D_MODEL, FFN_TF), lambda i, j: (0, 0, j)),
        pl.BlockSpec((None, D_MODEL, FFN_TF), lambda i, j: (1, 0, j)),
        pl.BlockSpec((FFN_TF, D_MODEL), lambda i, j: (j, 0)),
    ]
    out_specs = [pl.BlockSpec((FFN_TM, D_MODEL), row)]
    out_shape = [jax.ShapeDtypeStruct((m, D_MODEL), F32)]
    args = [x, shift, scale, gate, ln_g, ln_b, w_gu, w_gu, w_d]
    side_cast = next_weights is not None
    if side_cast:
        assert n_tiles * n_chunks * SIDE_ROWS_IN >= D_MODEL
        assert n_tiles * n_chunks * SIDE_ROWS_OUT == D_FF_PAD and D_FF % SIDE_ROWS_OUT == 0
        step = lambda i, j: i * n_chunks + j
        last_in = D_MODEL // SIDE_ROWS_IN - 1
        last_out = D_FF // SIDE_ROWS_OUT - 1
        in_specs += [
            pl.BlockSpec((SIDE_ROWS_IN, 2 * D_FF), lambda i, j: (jnp.minimum(step(i, j), last_in), 0)),
            pl.BlockSpec((SIDE_ROWS_OUT, D_MODEL), lambda i, j: (jnp.minimum(step(i, j), last_out), 0)),
        ]
        out_specs += [
            pl.BlockSpec((2, SIDE_ROWS_IN, D_FF_PAD),
                         lambda i, j: (0, jnp.minimum(step(i, j), last_in), 0)),
            pl.BlockSpec((SIDE_ROWS_OUT, D_MODEL), lambda i, j: (step(i, j), 0)),
        ]
        out_shape += [jax.ShapeDtypeStruct((2, D_MODEL, D_FF_PAD), BF16),
                      jax.ShapeDtypeStruct((D_FF_PAD, D_MODEL), BF16)]
        args += list(next_weights)
    outs = pl.pallas_call(
        functools.partial(_ffn_kernel, alpha=alpha, side_cast=side_cast),
        grid=(n_tiles, n_chunks),
        in_specs=in_specs,
        out_specs=out_specs,
        out_shape=out_shape,
        scratch_shapes=[pltpu.VMEM((FFN_TM, D_MODEL), BF16)],
        compiler_params=_params("arbitrary" if side_cast else "parallel", "arbitrary"),
        name="ffn_sublayer",
    )(*args)
    return outs if side_cast else outs[0]


def _rope(t, cosf, sina, sinb):
    up = pltpu.roll(t, LANES - ROT_HALF, axis=1)
    dn = pltpu.roll(t, ROT_HALF, axis=1)
    return t * cosf + up * sina + dn * sinb


def _rope_wide(t, cosf, sina, sinb):
    slabs = [_rope(t[:, s:s + LANES], cosf, sina, sinb)
             for s in range(0, t.shape[1], LANES)]
    return jnp.concatenate(slabs, axis=1)


def _proj_kernel(x_ref, shift_ref, scale_ref, w_ref, b_ref, cos_ref, sina_ref, sinb_ref,
                 xp_ref, q_ref, k_ref, v_ref, sa_ref, sb_ref, u_scr):
    scale1 = 1.0 + scale_ref[0]
    shift = shift_ref[0]
    q_off = POOL_WIDTH
    kv_off = q_off + ATTN_WIDTH
    ga_off = kv_off + 2 * KV_WIDTH
    gb_off = ga_off + GATE_WIDTH
    for start in range(0, PROJ_TM, PROJ_MC):
        rows = slice(start, start + PROJ_MC)
        for s in range(start, start + PROJ_MC, ROW_CHUNK):
            sub = slice(s, s + ROW_CHUNK)
            u_scr[sub, :] = (x_ref[sub, :] * scale1 + shift).astype(BF16)
        u = u_scr[rows, :]
        cosf = cos_ref[rows, :]
        sina = sina_ref[rows, :]
        sinb = sinb_ref[rows, :]
        for col in range(0, IN_WIDTH, PROJ_TN):
            cols = slice(col, col + PROJ_TN)
            h = jnp.dot(u, w_ref[:, cols], preferred_element_type=F32) + b_ref[:, cols]
            if col < q_off:
                xp_ref[rows, cols] = h
            elif col < kv_off:
                q = _rope_wide(h, cosf, sina, sinb) * (HEAD_DIM ** -0.5)
                q_ref[rows, col - q_off:col - q_off + PROJ_TN] = q.astype(BF16)
            elif col < ga_off:
                k_ref[rows, :] = _rope_wide(h[:, :KV_WIDTH], cosf, sina, sinb).astype(BF16)
                v_ref[rows, :] = h[:, KV_WIDTH:].astype(BF16)
            elif col < gb_off:
                sa_ref[rows, col - ga_off:col - ga_off + PROJ_TN] = jax.nn.sigmoid(h).astype(BF16)
            else:
                sb_ref[rows, col - gb_off:col - gb_off + PROJ_TN] = jax.nn.sigmoid(h).astype(BF16)


def _input_projection(x, shift, scale, w_in, b_in, cosf, sina, sinb, *, seq):
    m = x.shape[0]
    tiles_per_seq = seq // PROJ_TM
    row = lambda i: (i, 0)
    per_batch = lambda i: (i // tiles_per_seq, 0, 0)
    const = lambda i: (0, 0)
    pos = lambda i: (i % tiles_per_seq, 0)
    out_widths = (POOL_WIDTH, ATTN_WIDTH, KV_WIDTH, KV_WIDTH, GATE_WIDTH, GATE_WIDTH)
    out_dtypes = (F32, BF16, BF16, BF16, BF16, BF16)
    return pl.pallas_call(
        _proj_kernel,
        grid=(m // PROJ_TM,),
        in_specs=[
            pl.BlockSpec((PROJ_TM, D_MODEL), row),
            pl.BlockSpec((1, 1, D_MODEL), per_batch),
            pl.BlockSpec((1, 1, D_MODEL), per_batch),
            _resident((D_MODEL, IN_WIDTH), const),
            _resident((1, IN_WIDTH), const),
            pl.BlockSpec((PROJ_TM, LANES), pos),
            pl.BlockSpec((PROJ_TM, LANES), pos),
            pl.BlockSpec((PROJ_TM, LANES), pos),
        ],
        out_specs=[pl.BlockSpec((PROJ_TM, w), row) for w in out_widths],
        out_shape=[jax.ShapeDtypeStruct((m, w), dt) for w, dt in zip(out_widths, out_dtypes)],
        scratch_shapes=[pltpu.VMEM((PROJ_TM, D_MODEL), BF16)],
        compiler_params=_params("parallel"),
        name="mix_in_proj",
    )(x, shift, scale, w_in, b_in, cosf, sina, sinb)


def _attn_kernel(sinks_ref, q_ref, kp_ref, kc_ref, vp_ref, vc_ref, o_ref):
    has_prev = pl.program_id(1) > 0
    kall = jnp.concatenate([kp_ref[...], kc_ref[...]], axis=0)
    vall = jnp.concatenate([vp_ref[...], vc_ref[...]], axis=0)
    r = lax.broadcasted_iota(jnp.int32, (2 * WINDOW, WINDOW), 0) & (WINDOW - 1)
    c = lax.broadcasted_iota(jnp.int32, (2 * WINDOW, WINDOW), 1)
    own = c <= r
    upper = lax.broadcasted_iota(jnp.int32, (2 * WINDOW, 1), 0) < WINDOW
    lo = lax.broadcasted_iota(jnp.int32, (2 * WINDOW, LANES), 1) < HEAD_DIM
    zero = jnp.zeros((2 * WINDOW, LANES), BF16)
    contract_last = (((1,), (1,)), ((), ()))
    for blk in range(ATTN_TQ // WINDOW):
        rows = slice(blk * WINDOW, (blk + 1) * WINDOW)
        keys = slice(blk * WINDOW, (blk + 2) * WINDOW)
        scores, values = [], []
        for slab in range(KV_WIDTH // LANES):
            lanes = slice(slab * LANES, (slab + 1) * LANES)
            ks, vs = kall[keys, lanes], vall[keys, lanes]
            ks_sw = pltpu.roll(ks, HEAD_DIM, axis=1)
            vs_sw = pltpu.roll(vs, HEAD_DIM, axis=1)
            for half in range(LANES // HEAD_DIM):
                kv = slab * (LANES // HEAD_DIM) + half
                if half == 0:
                    k_lo, k_hi = jnp.where(lo, ks, zero), jnp.where(lo, zero, ks_sw)
                    v_lo, v_hi = jnp.where(lo, vs, zero), jnp.where(lo, zero, vs_sw)
                else:
                    k_lo, k_hi = jnp.where(lo, ks_sw, zero), jnp.where(lo, zero, ks)
                    v_lo, v_hi = jnp.where(lo, vs_sw, zero), jnp.where(lo, zero, vs)
                base = kv * Q_GROUP * HEAD_DIM
                q2 = jnp.concatenate([q_ref[rows, base:base + LANES],
                                      q_ref[rows, base + LANES:base + 2 * LANES]], axis=0)
                for k_x, v_x in ((k_lo, v_lo), (k_hi, v_hi)):
                    scores.append(lax.dot_general(q2, k_x, contract_last,
                                                  preferred_element_type=F32))
                    values.append(v_x)
        probs = []
        for t, s in enumerate(scores):
            head = (t // 2) * Q_GROUP + (t % 2)
            s_prev, s_own = s[:, :WINDOW], s[:, WINDOW:]
            if blk == 0:
                s_prev = jnp.where(has_prev, s_prev, MASK_VALUE)
            dense = jnp.where(own, s_own, s_prev)
            sink = jnp.where(upper, sinks_ref[head], sinks_ref[head + 2])
            m = jnp.maximum(jnp.max(dense, axis=-1, keepdims=True), sink)
            p = jnp.exp(dense - m)
            denom = jnp.sum(p, axis=-1, keepdims=True) + jnp.exp(sink - m)
            p = (p * (1.0 / denom)).astype(BF16)
            pz = jnp.zeros_like(p)
            probs.append(jnp.concatenate([jnp.where(own, pz, p), jnp.where(own, p, pz)], axis=1))
        for kv in range(N_KV_HEADS):
            base = kv * Q_GROUP * HEAD_DIM
            out = (jnp.dot(probs[2 * kv], values[2 * kv], preferred_element_type=F32)
                   + jnp.dot(probs[2 * kv + 1], values[2 * kv + 1], preferred_element_type=F32))
            out = out.astype(BF16)
            o_ref[rows, base:base + LANES] = out[:WINDOW]
            o_ref[rows, base + LANES:base + 2 * LANES] = out[WINDOW:]


def _attention(q, k, v, sinks, *, batch, seq):
    m = q.shape[0]
    nt = seq // ATTN_TQ
    halo_blocks = ATTN_TQ // WINDOW
    cur = lambda b, i: (b * nt + i, 0)
    prev = lambda b, i: (jnp.maximum((b * nt + i) * halo_blocks - 1, 0), 0)
    return pl.pallas_call(
        _attn_kernel,
        grid=(batch, nt),
        in_specs=[
            pl.BlockSpec(memory_space=pltpu.SMEM),
            pl.BlockSpec((ATTN_TQ, ATTN_WIDTH), cur),
            pl.BlockSpec((WINDOW, KV_WIDTH), prev),
            pl.BlockSpec((ATTN_TQ, KV_WIDTH), cur),
            pl.BlockSpec((WINDOW, KV_WIDTH), prev),
            pl.BlockSpec((ATTN_TQ, KV_WIDTH), cur),
        ],
        out_specs=pl.BlockSpec((ATTN_TQ, ATTN_WIDTH), cur),
        out_shape=jax.ShapeDtypeStruct((m, ATTN_WIDTH), BF16),
        compiler_params=_params("parallel", "parallel"),
        name="swa_attention",
    )(sinks, q, k, k, v, v)


def _merge_kernel(xp_ref, xprev_ref, attn_ref, sa_ref, sb_ref, x_ref, gate_ref, g_ref, b_ref,
                  wpool_ref, pscale_ref, wa_ref, wb_ref, wout_ref, o_ref, *, alpha, tiles_per_seq):
    tile_in_seq = pl.program_id(0) % tiles_per_seq
    gate1 = 1.0 + gate_ref[0]
    for start in range(0, MERGE_TM, MERGE_MC):
        rows = slice(start, start + MERGE_MC)
        xp = xp_ref[rows, :]
        if start == 0:
            halo = jnp.where(tile_in_seq > 0, xprev_ref[...], 0.0)
        else:
            halo = xp_ref[start - POOL_HALO:start, :]
        xfull = jnp.concatenate([halo, xp], axis=0)
        t1 = (tile_in_seq * MERGE_TM + start + 1
              + lax.broadcasted_iota(jnp.int32, (MERGE_MC, 1), 0))
        mixed = []
        for gi, w in enumerate(POOL_WINDOWS):
            cols = slice(gi * POOL_GROUP, (gi + 1) * POOL_GROUP)
            s = xfull[:, cols]
            span = 1
            while span < w:
                s = s + pltpu.roll(s, span, axis=0)
                span *= 2
            count = jnp.minimum(t1, w).astype(F32)
            pooled = s[POOL_HALO:] / count - xp[:, cols]
            mg = jnp.dot(pooled.astype(BF16), wpool_ref[gi], preferred_element_type=F32)
            mixed.append((mg * pscale_ref[:, cols]).astype(BF16))
        mixed = jnp.concatenate(mixed, axis=1)
        y_a = jnp.dot(mixed, wa_ref[...], preferred_element_type=F32)
        y_b = jnp.dot(attn_ref[rows, :], wb_ref[...], preferred_element_type=F32)
        merged = (sa_ref[rows, :].astype(F32) * y_a
                  + sb_ref[rows, :].astype(F32) * y_b).astype(BF16)
        y = jnp.dot(merged, wout_ref[...], preferred_element_type=F32)
        z = alpha * x_ref[rows, :] + gate1 * y
        o_ref[rows, :] = _layer_norm(z, g_ref[...], b_ref[...])


def _merge_sublayer(xp, attn, sa, sb, x, gate, ln_g, ln_b, w_pool, pool_scale,
                    w_a, w_b, w_out, *, seq, alpha):
    m = x.shape[0]
    tiles_per_seq = seq // MERGE_TM
    halo_blocks = MERGE_TM // POOL_HALO
    row = lambda i: (i, 0)
    prev_rows = lambda i: (jnp.maximum(i * halo_blocks - 1, 0), 0)
    per_batch = lambda i: (i // tiles_per_seq, 0, 0)
    const = lambda i: (0, 0)
    const3 = lambda i: (0, 0, 0)
    return pl.pallas_call(
        functools.partial(_merge_kernel, alpha=alpha, tiles_per_seq=tiles_per_seq),
        grid=(m // MERGE_TM,),
        in_specs=[
            pl.BlockSpec((MERGE_TM, POOL_WIDTH), row),
            pl.BlockSpec((POOL_HALO, POOL_WIDTH), prev_rows),
            pl.BlockSpec((MERGE_TM, ATTN_WIDTH), row),
            pl.BlockSpec((MERGE_TM, GATE_WIDTH), row),
            pl.BlockSpec((MERGE_TM, GATE_WIDTH), row),
            pl.BlockSpec((MERGE_TM, D_MODEL), row),
            pl.BlockSpec((1, 1, D_MODEL), per_batch),
            pl.BlockSpec((1, D_MODEL), const),
            pl.BlockSpec((1, D_MODEL), const),
            _resident((len(POOL_WINDOWS), POOL_GROUP, POOL_GROUP), const3),
            _resident((1, POOL_WIDTH), const),
            _resident((POOL_WIDTH, D_MODEL), const),
            _resident((ATTN_WIDTH, D_MODEL), const),
            _resident((D_MODEL, D_MODEL), const),
        ],
        out_specs=pl.BlockSpec((MERGE_TM, D_MODEL), row),
        out_shape=jax.ShapeDtypeStruct((m, D_MODEL), F32),
        compiler_params=_params("parallel"),
        name="mix_merge",
    )(xp, xp, attn, sa, sb, x, gate, ln_g, ln_b, w_pool, pool_scale, w_a, w_b, w_out)


def _rope_tables(seq):
    pos = jnp.arange(seq, dtype=F32)
    inv_freq = ROPE_THETA ** (-jnp.arange(0, ROT_DIM, 2, dtype=F32) / ROT_DIM)
    ang = pos[:, None] * inv_freq[None, :]
    cos, sin = jnp.cos(ang), jnp.sin(ang)
    d = np.arange(LANES) % HEAD_DIM
    idx = d % ROT_HALF
    lo = jnp.asarray(d < ROT_HALF)
    hi = jnp.asarray((d >= ROT_HALF) & (d < ROT_DIM))
    cosf = jnp.where(lo | hi, cos[:, idx], 1.0)
    sina = jnp.where(lo, -sin[:, idx], 0.0)
    sinb = jnp.where(hi, sin[:, idx], 0.0)
    return cosf, sina, sinb


def kernel(x, c, w_ada, b_ada, ln_g, ln_b, w_ffn1_in, w_ffn1_out, w_in, b_in, w_pool,
           pool_scale, sinks, w_branch_a, w_branch_b, w_out, w_ffn2_in, w_ffn2_out):
    batch, seq, d = x.shape
    depth = w_ada.shape[0]
    assert d == D_MODEL and batch <= MOD_ROWS
    assert seq % FFN_TM == 0 and seq % PROJ_TM == 0 and seq % MERGE_TM == 0 and seq % WINDOW == 0
    alpha = (2 * depth) ** 0.25
    cosf, sina, sinb = _rope_tables(seq)
    c_pad = jnp.pad(c, ((0, MOD_ROWS - batch), (0, 0)))
    h = x.reshape(batch * seq, d)

    for l in range(depth):
        mod = _modulation(c_pad, w_ada[l], b_ada[l])[:batch]
        mod = mod.reshape(batch, N_SUBLAYERS, 3, 1, d)
        shift = lambda s: mod[:, s, 0]
        scale = lambda s: mod[:, s, 1]
        gate = lambda s: mod[:, s, 2]
        g = lambda s: ln_g[l, s].reshape(1, d)
        b = lambda s: ln_b[l, s].reshape(1, d)

        w_gu, w_d = _ffn_weights(w_ffn1_in[l], w_ffn1_out[l])
        h, w_gu2, w_d2 = _ffn_sublayer(h, shift(0), scale(0), gate(0), g(0), b(0), w_gu, w_d,
                                       seq=seq, alpha=alpha,
                                       next_weights=(w_ffn2_in[l], w_ffn2_out[l]))

        xp, q, k, v, sa, sb = _input_projection(
            h, shift(1), scale(1), w_in[l].astype(BF16), b_in[l].reshape(1, IN_WIDTH),
            cosf, sina, sinb, seq=seq)
        attn = _attention(q, k, v, sinks[l], batch=batch, seq=seq)
        h = _merge_sublayer(
            xp, attn, sa, sb, h, gate(1), g(1), b(1), w_pool[l].astype(BF16),
            pool_scale[l].reshape(1, POOL_WIDTH), w_branch_a[l].astype(BF16),
            w_branch_b[l].astype(BF16), w_out[l].astype(BF16), seq=seq, alpha=alpha)

        h = _ffn_sublayer(h, shift(2), scale(2), gate(2), g(2), b(2), w_gu2, w_d2,
                          seq=seq, alpha=alpha)
    return h.reshape(batch, seq, d)
```

```python
import functools

import jax
import jax.numpy as jnp
import numpy as np
from jax import lax
from jax.experimental import pallas as pl
from jax.experimental.pallas import tpu as pltpu

F32 = jnp.float32
BF16 = jnp.bfloat16

D_MODEL = 2048
N_Q_HEADS = 16
N_KV_HEADS = 4
HEAD_DIM = 64
Q_GROUP = N_Q_HEADS // N_KV_HEADS
ATTN_WIDTH = N_Q_HEADS * HEAD_DIM
KV_WIDTH = N_KV_HEADS * HEAD_DIM
WINDOW = 128
ROPE_THETA = 500000.0
ROT_DIM = HEAD_DIM // 4
ROT_HALF = ROT_DIM // 2
POOL_WINDOWS = (2, 4, 8, 16)
POOL_WIDTH = D_MODEL // 2
POOL_GROUP = POOL_WIDTH // len(POOL_WINDOWS)
POOL_HALO = 16
GATE_WIDTH = D_MODEL
IN_WIDTH = POOL_WIDTH + ATTN_WIDTH + 2 * KV_WIDTH + 2 * GATE_WIDTH
D_FF = 5504
N_SUBLAYERS = 3
LN_EPS = 1e-5
MASK_VALUE = -1e30

LANES = 128
VMEM_LIMIT = 60 * 1024 * 1024
FFN_TM = 1024
FFN_MC = 512
FFN_EDGE_MC = 256
FFN_TF = 512
D_FF_PAD = ((D_FF + FFN_TF - 1) // FFN_TF) * FFN_TF
CAST_ROWS_IN = 128
CAST_ROWS_OUT = 128
CAST_OUT_PARTS = 4
SIDE_ROWS_IN = 16
SIDE_ROWS_OUT = 32
assert D_FF % LANES == 0 and D_FF % CAST_ROWS_OUT == 0
assert D_FF_PAD % (CAST_ROWS_OUT * CAST_OUT_PARTS) == 0
PROJ_TM = 512
PROJ_MC = 256
PROJ_TN = 512
ATTN_TQ = 512
MERGE_TM = 512
MERGE_MC = 256
ROW_CHUNK = 128
MOD_TN = 1024
MOD_ROWS = 8


def _params(*sem):
    return pltpu.CompilerParams(dimension_semantics=sem, vmem_limit_bytes=VMEM_LIMIT)


def _resident(shape, index_map):
    return pl.BlockSpec(shape, index_map, pipeline_mode=pl.Buffered(1))


def _layer_norm(z, g, b):
    mu = jnp.mean(z, axis=-1, keepdims=True)
    zc = z - mu
    var = jnp.mean(zc * zc, axis=-1, keepdims=True)
    return zc * lax.rsqrt(var + LN_EPS) * g + b


def _mod_kernel(c_ref, w_ref, b_ref, o_ref):
    c = c_ref[...]
    c_act = (c * jax.nn.sigmoid(c)).astype(BF16)
    o_ref[...] = jnp.dot(c_act, w_ref[...].astype(BF16),
                         preferred_element_type=F32) + b_ref[...]


def _modulation(c_pad, w_ada, b_ada):
    n = w_ada.shape[1]
    return pl.pallas_call(
        _mod_kernel,
        grid=(n // MOD_TN,),
        in_specs=[
            pl.BlockSpec((MOD_ROWS, D_MODEL), lambda j: (0, 0)),
            pl.BlockSpec((D_MODEL, MOD_TN), lambda j: (0, j)),
            pl.BlockSpec((1, MOD_TN), lambda j: (0, j)),
        ],
        out_specs=pl.BlockSpec((MOD_ROWS, MOD_TN), lambda j: (0, j)),
        out_shape=jax.ShapeDtypeStruct((MOD_ROWS, n), F32),
        compiler_params=_params("arbitrary"),
        name="adaln_mod",
    )(c_pad, w_ada, b_ada.reshape(1, n))


def _cast_in_kernel(w_ref, o_ref):
    o_ref[0, :, :D_FF] = w_ref[:, :D_FF].astype(BF16)
    o_ref[1, :, :D_FF] = w_ref[:, D_FF:].astype(BF16)
    o_ref[:, :, D_FF:] = jnp.zeros((2, o_ref.shape[1], D_FF_PAD - D_FF), BF16)


def _cast_out_kernel(*refs):
    w_refs, o_ref = refs[:-1], refs[-1]
    j = pl.program_id(0)
    for t, w_ref in enumerate(w_refs):
        rows = slice(t * CAST_ROWS_OUT, (t + 1) * CAST_ROWS_OUT)
        in_range = j * CAST_OUT_PARTS + t < D_FF // CAST_ROWS_OUT

        @pl.when(in_range)
        def _():
            o_ref[rows, :] = w_ref[...].astype(BF16)

        @pl.when(jnp.logical_not(in_range))
        def _():
            o_ref[rows, :] = jnp.zeros((CAST_ROWS_OUT, D_MODEL), BF16)


def _ffn_weights(w_in, w_out):
    w_gu = pl.pallas_call(
        _cast_in_kernel,
        grid=(D_MODEL // CAST_ROWS_IN,),
        in_specs=[pl.BlockSpec((CAST_ROWS_IN, 2 * D_FF), lambda i: (i, 0))],
        out_specs=pl.BlockSpec((2, CAST_ROWS_IN, D_FF_PAD), lambda i: (0, i, 0)),
        out_shape=jax.ShapeDtypeStruct((2, D_MODEL, D_FF_PAD), BF16),
        compiler_params=_params("parallel"),
        name="cast_ffn_in",
    )(w_in)
    last_in = D_FF // CAST_ROWS_OUT - 1
    part = lambda t: pl.BlockSpec(
        (CAST_ROWS_OUT, D_MODEL), lambda j: (jnp.minimum(j * CAST_OUT_PARTS + t, last_in), 0))
    w_d = pl.pallas_call(
        _cast_out_kernel,
        grid=(D_FF_PAD // (CAST_ROWS_OUT * CAST_OUT_PARTS),),
        in_specs=[part(t) for t in range(CAST_OUT_PARTS)],
        out_specs=pl.BlockSpec((CAST_ROWS_OUT * CAST_OUT_PARTS, D_MODEL), lambda j: (j, 0)),
        out_shape=jax.ShapeDtypeStruct((D_FF_PAD, D_MODEL), BF16),
        compiler_params=_params("parallel"),
        name="cast_ffn_out",
    )(*([w_out] * CAST_OUT_PARTS))
    return w_gu, w_d


def _swiglu_chain(u, wg_ref, wu_ref, wd_ref):
    a = jnp.dot(u, wg_ref[...], preferred_element_type=F32)
    b = jnp.dot(u, wu_ref[...], preferred_element_type=F32)
    h = (a * jax.nn.sigmoid(a) * b).astype(BF16)
    return jnp.dot(h, wd_ref[...], preferred_element_type=F32)


def _ffn_kernel(x_ref, shift_ref, scale_ref, gate_ref, g_ref, b_ref, wg_ref, wu_ref, wd_ref,
                *rest, alpha, side_cast):
    j = pl.program_id(1)
    last = pl.num_programs(1) - 1
    if side_cast:
        (next_in_ref, next_out_ref, plain_ref, o_ref,
         cast_in_ref, cast_out_ref, cast_plain_ref, u_scr) = rest
        step = pl.program_id(0) * pl.num_programs(1) + j

        def side_work():
            _cast_in_kernel(next_in_ref, cast_in_ref)
            cast_out_ref[...] = jnp.where(step < D_FF // SIDE_ROWS_OUT,
                                          next_out_ref[...], 0.0).astype(BF16)
            cast_plain_ref[...] = plain_ref[...].astype(BF16)
    else:
        o_ref, u_scr = rest

        def side_work():
            pass

    @pl.when(j == 0)
    def _():
        scale1 = 1.0 + scale_ref[0]
        shift = shift_ref[0]
        for r in range(FFN_TM // FFN_EDGE_MC):
            rows = slice(r * FFN_EDGE_MC, (r + 1) * FFN_EDGE_MC)
            for s in range(r * FFN_EDGE_MC, (r + 1) * FFN_EDGE_MC, ROW_CHUNK):
                sub = slice(s, s + ROW_CHUNK)
                u_scr[sub, :] = (x_ref[sub, :] * scale1 + shift).astype(BF16)
            o_ref[rows, :] = _swiglu_chain(u_scr[rows, :], wg_ref, wu_ref, wd_ref)
        side_work()

    @pl.when((j > 0) & (j < last))
    def _():
        for r in range(FFN_TM // FFN_MC):
            rows = slice(r * FFN_MC, (r + 1) * FFN_MC)
            o_ref[rows, :] += _swiglu_chain(u_scr[rows, :], wg_ref, wu_ref, wd_ref)
        side_work()

    @pl.when(j == last)
    def _():
        gate_half = 0.5 * (1.0 + gate_ref[0])
        g = g_ref[...]
        beta = b_ref[...]

        def finish(r):
            for s in range(r * FFN_EDGE_MC, (r + 1) * FFN_EDGE_MC, ROW_CHUNK):
                sub = slice(s, s + ROW_CHUNK)
                z = alpha * x_ref[sub, :] + gate_half * o_ref[sub, :]
                o_ref[sub, :] = _layer_norm(z, g, beta)

        n_chains = FFN_TM // FFN_EDGE_MC
        for r in range(n_chains):
            rows = slice(r * FFN_EDGE_MC, (r + 1) * FFN_EDGE_MC)
            o_ref[rows, :] += _swiglu_chain(u_scr[rows, :], wg_ref, wu_ref, wd_ref)
            if r > 0:
                finish(r - 1)
        finish(n_chains - 1)
        side_work()


def _ffn_sublayer(x, shift, scale, gate, ln_g, ln_b, w_gu, w_d, *, seq, alpha, next_weights=None):
    m = x.shape[0]
    tiles_per_seq = seq // FFN_TM
    n_tiles, n_chunks = m // FFN_TM, D_FF_PAD // FFN_TF
    row = lambda i, j: (i, 0)
    per_batch = lambda i, j: (i // tiles_per_seq, 0, 0)
    const = lambda i, j: (0, 0)
    in_specs = [
        pl.BlockSpec((FFN_TM, D_MODEL), row),
        pl.BlockSpec((1, 1, D_MODEL), per_batch),
        pl.BlockSpec((1, 1, D_MODEL), per_batch),
        pl.BlockSpec((1, 1, D_MODEL), per_batch),
        pl.BlockSpec((1, D_MODEL), const),
        pl.BlockSpec((1, D_MODEL), const),
        pl.BlockSpec((None, D_MODEL, FFN_TF), lambda i, j: (0, 0, j)),
        pl.BlockSpec((None, D_MODEL, FFN_TF), lambda i, j: (1, 0, j)),
        pl.BlockSpec((FFN_TF, D_MODEL), lambda i, j: (j, 0)),
    ]
    out_specs = [pl.BlockSpec((FFN_TM, D_MODEL), row)]
    out_shape = [jax.ShapeDtypeStruct((m, D_MODEL), F32)]
    args = [x, shift, scale, gate, ln_g, ln_b, w_gu, w_gu, w_d]
    side_cast = next_weights is not None
    if side_cast:
        assert n_tiles * n_chunks * SIDE_ROWS_IN >= D_MODEL
        assert n_tiles * n_chunks * SIDE_ROWS_OUT == D_FF_PAD and D_FF % SIDE_ROWS_OUT == 0
        step = lambda i, j: i * n_chunks + j
        last_in = D_MODEL // SIDE_ROWS_IN - 1
        last_out = D_FF // SIDE_ROWS_OUT - 1
        n_plain = next_weights[2].shape[1]
        in_specs += [
            pl.BlockSpec((SIDE_ROWS_IN, 2 * D_FF), lambda i, j: (jnp.minimum(step(i, j), last_in), 0)),
            pl.BlockSpec((SIDE_ROWS_OUT, D_MODEL), lambda i, j: (jnp.minimum(step(i, j), last_out), 0)),
            pl.BlockSpec((SIDE_ROWS_IN, n_plain), lambda i, j: (jnp.minimum(step(i, j), last_in), 0)),
        ]
        out_specs += [
            pl.BlockSpec((2, SIDE_ROWS_IN, D_FF_PAD),
                         lambda i, j: (0, jnp.minimum(step(i, j), last_in), 0)),
            pl.BlockSpec((SIDE_ROWS_OUT, D_MODEL), lambda i, j: (step(i, j), 0)),
            pl.BlockSpec((SIDE_ROWS_IN, n_plain), lambda i, j: (jnp.minimum(step(i, j), last_in), 0)),
        ]
        out_shape += [jax.ShapeDtypeStruct((2, D_MODEL, D_FF_PAD), BF16),
                      jax.ShapeDtypeStruct((D_FF_PAD, D_MODEL), BF16),
                      jax.ShapeDtypeStruct((D_MODEL, n_plain), BF16)]
        args += list(next_weights)
    outs = pl.pallas_call(
        functools.partial(_ffn_kernel, alpha=alpha, side_cast=side_cast),
        grid=(n_tiles, n_chunks),
        in_specs=in_specs,
        out_specs=out_specs,
        out_shape=out_shape,
        scratch_shapes=[pltpu.VMEM((FFN_TM, D_MODEL), BF16)],
        compiler_params=_params("arbitrary" if side_cast else "parallel", "arbitrary"),
        name="ffn_sublayer",
    )(*args)
    return outs if side_cast else outs[0]


def _rope(t, cosf, sina, sinb):
    up = pltpu.roll(t, LANES - ROT_HALF, axis=1)
    dn = pltpu.roll(t, ROT_HALF, axis=1)
    return t * cosf + up * sina + dn * sinb


def _rope_wide(t, cosf, sina, sinb):
    slabs = [_rope(t[:, s:s + LANES], cosf, sina, sinb)
             for s in range(0, t.shape[1], LANES)]
    return jnp.concatenate(slabs, axis=1)


def _proj_kernel(x_ref, shift_ref, scale_ref, w_ref, b_ref, cos_ref, sina_ref, sinb_ref,
                 xp_ref, q_ref, k_ref, v_ref, sa_ref, sb_ref, u_scr):
    scale1 = 1.0 + scale_ref[0]
    shift = shift_ref[0]
    q_off = POOL_WIDTH
    kv_off = q_off + ATTN_WIDTH
    ga_off = kv_off + 2 * KV_WIDTH
    gb_off = ga_off + GATE_WIDTH
    for start in range(0, PROJ_TM, PROJ_MC):
        rows = slice(start, start + PROJ_MC)
        for s in range(start, start + PROJ_MC, ROW_CHUNK):
            sub = slice(s, s + ROW_CHUNK)
            u_scr[sub, :] = (x_ref[sub, :] * scale1 + shift).astype(BF16)
        u = u_scr[rows, :]
        cosf = cos_ref[rows, :]
        sina = sina_ref[rows, :]
        sinb = sinb_ref[rows, :]
        for col in range(0, IN_WIDTH, PROJ_TN):
            cols = slice(col, col + PROJ_TN)
            h = jnp.dot(u, w_ref[:, cols], preferred_element_type=F32) + b_ref[:, cols]
            if col < q_off:
                xp_ref[rows, cols] = h
            elif col < kv_off:
                q = _rope_wide(h, cosf, sina, sinb) * (HEAD_DIM ** -0.5)
                q_ref[rows, col - q_off:col - q_off + PROJ_TN] = q.astype(BF16)
            elif col < ga_off:
                k_ref[rows, :] = _rope_wide(h[:, :KV_WIDTH], cosf, sina, sinb).astype(BF16)
                v_ref[rows, :] = h[:, KV_WIDTH:].astype(BF16)
            elif col < gb_off:
                sa_ref[rows, col - ga_off:col - ga_off + PROJ_TN] = jax.nn.sigmoid(h).astype(BF16)
            else:
                sb_ref[rows, col - gb_off:col - gb_off + PROJ_TN] = jax.nn.sigmoid(h).astype(BF16)


def _input_projection(x, shift, scale, w_in, b_in, cosf, sina, sinb, *, seq):
    m = x.shape[0]
    tiles_per_seq = seq // PROJ_TM
    row = lambda i: (i, 0)
    per_batch = lambda i: (i // tiles_per_seq, 0, 0)
    const = lambda i: (0, 0)
    pos = lambda i: (i % tiles_per_seq, 0)
    out_widths = (POOL_WIDTH, ATTN_WIDTH, KV_WIDTH, KV_WIDTH, GATE_WIDTH, GATE_WIDTH)
    out_dtypes = (F32, BF16, BF16, BF16, BF16, BF16)
    return pl.pallas_call(
        _proj_kernel,
        grid=(m // PROJ_TM,),
        in_specs=[
            pl.BlockSpec((PROJ_TM, D_MODEL), row),
            pl.BlockSpec((1, 1, D_MODEL), per_batch),
            pl.BlockSpec((1, 1, D_MODEL), per_batch),
            _resident((D_MODEL, IN_WIDTH), const),
            _resident((1, IN_WIDTH), const),
            pl.BlockSpec((PROJ_TM, LANES), pos),
            pl.BlockSpec((PROJ_TM, LANES), pos),
            pl.BlockSpec((PROJ_TM, LANES), pos),
        ],
        out_specs=[pl.BlockSpec((PROJ_TM, w), row) for w in out_widths],
        out_shape=[jax.ShapeDtypeStruct((m, w), dt) for w, dt in zip(out_widths, out_dtypes)],
        scratch_shapes=[pltpu.VMEM((PROJ_TM, D_MODEL), BF16)],
        compiler_params=_params("parallel"),
        name="mix_in_proj",
    )(x, shift, scale, w_in, b_in, cosf, sina, sinb)


def _attn_kernel(sinks_ref, q_ref, kp_ref, kc_ref, vp_ref, vc_ref, o_ref):
    has_prev = pl.program_id(1) > 0
    kall = jnp.concatenate([kp_ref[...], kc_ref[...]], axis=0)
    vall = jnp.concatenate([vp_ref[...], vc_ref[...]], axis=0)
    r = lax.broadcasted_iota(jnp.int32, (2 * WINDOW, WINDOW), 0) & (WINDOW - 1)
    c = lax.broadcasted_iota(jnp.int32, (2 * WINDOW, WINDOW), 1)
    own = c <= r
    upper = lax.broadcasted_iota(jnp.int32, (2 * WINDOW, 1), 0) < WINDOW
    lo = lax.broadcasted_iota(jnp.int32, (2 * WINDOW, LANES), 1) < HEAD_DIM
    zero = jnp.zeros((2 * WINDOW, LANES), BF16)
    contract_last = (((1,), (1,)), ((), ()))
    for blk in range(ATTN_TQ // WINDOW):
        rows = slice(blk * WINDOW, (blk + 1) * WINDOW)
        keys = slice(blk * WINDOW, (blk + 2) * WINDOW)
        scores, values = [], []
        for slab in range(KV_WIDTH // LANES):
            lanes = slice(slab * LANES, (slab + 1) * LANES)
            ks, vs = kall[keys, lanes], vall[keys, lanes]
            ks_sw = pltpu.roll(ks, HEAD_DIM, axis=1)
            vs_sw = pltpu.roll(vs, HEAD_DIM, axis=1)
            for half in range(LANES // HEAD_DIM):
                kv = slab * (LANES // HEAD_DIM) + half
                if half == 0:
                    k_lo, k_hi = jnp.where(lo, ks, zero), jnp.where(lo, zero, ks_sw)
                    v_lo, v_hi = jnp.where(lo, vs, zero), jnp.where(lo, zero, vs_sw)
                else:
                    k_lo, k_hi = jnp.where(lo, ks_sw, zero), jnp.where(lo, zero, ks)
                    v_lo, v_hi = jnp.where(lo, vs_sw, zero), jnp.where(lo, zero, vs)
                base = kv * Q_GROUP * HEAD_DIM
                q2 = jnp.concatenate([q_ref[rows, base:base + LANES],
                                      q_ref[rows, base + LANES:base + 2 * LANES]], axis=0)
                for k_x, v_x in ((k_lo, v_lo), (k_hi, v_hi)):
                    scores.append(lax.dot_general(q2, k_x, contract_last,
                                                  preferred_element_type=F32))
                    values.append(v_x)
        probs = []
        for t, s in enumerate(scores):
            head = (t // 2) * Q_GROUP + (t % 2)
            s_prev, s_own = s[:, :WINDOW], s[:, WINDOW:]
            if blk == 0:
                s_prev = jnp.where(has_prev, s_prev, MASK_VALUE)
            dense = jnp.where(own, s_own, s_prev)
            sink = jnp.where(upper, sinks_ref[head], sinks_ref[head + 2])
            m = jnp.maximum(jnp.max(dense, axis=-1, keepdims=True), sink)
            p = jnp.exp(dense - m)
            denom = jnp.sum(p, axis=-1, keepdims=True) + jnp.exp(sink - m)
            p = (p * (1.0 / denom)).astype(BF16)
            pz = jnp.zeros_like(p)
            probs.append(jnp.concatenate([jnp.where(own, pz, p), jnp.where(own, p, pz)], axis=1))
        for kv in range(N_KV_HEADS):
            base = kv * Q_GROUP * HEAD_DIM
            out = (jnp.dot(probs[2 * kv], values[2 * kv], preferred_element_type=F32)
                   + jnp.dot(probs[2 * kv + 1], values[2 * kv + 1], preferred_element_type=F32))
            out = out.astype(BF16)
            o_ref[rows, base:base + LANES] = out[:WINDOW]
            o_ref[rows, base + LANES:base + 2 * LANES] = out[WINDOW:]


def _attention(q, k, v, sinks, *, batch, seq):
    m = q.shape[0]
    nt = seq // ATTN_TQ
    halo_blocks = ATTN_TQ // WINDOW
    cur = lambda b, i: (b * nt + i, 0)
    prev = lambda b, i: (jnp.maximum((b * nt + i) * halo_blocks - 1, 0), 0)
    return pl.pallas_call(
        _attn_kernel,
        grid=(batch, nt),
        in_specs=[
            pl.BlockSpec(memory_space=pltpu.SMEM),
            pl.BlockSpec((ATTN_TQ, ATTN_WIDTH), cur),
            pl.BlockSpec((WINDOW, KV_WIDTH), prev),
            pl.BlockSpec((ATTN_TQ, KV_WIDTH), cur),
            pl.BlockSpec((WINDOW, KV_WIDTH), prev),
            pl.BlockSpec((ATTN_TQ, KV_WIDTH), cur),
        ],
        out_specs=pl.BlockSpec((ATTN_TQ, ATTN_WIDTH), cur),
        out_shape=jax.ShapeDtypeStruct((m, ATTN_WIDTH), BF16),
        compiler_params=_params("parallel", "parallel"),
        name="swa_attention",
    )(sinks, q, k, k, v, v)


def _merge_kernel(xp_ref, xprev_ref, attn_ref, sa_ref, sb_ref, x_ref, gate_ref, g_ref, b_ref,
                  wpool_ref, pscale_ref, wa_ref, wb_ref, wout_ref, o_ref, *, alpha, tiles_per_seq):
    tile_in_seq = pl.program_id(0) % tiles_per_seq
    gate1 = 1.0 + gate_ref[0]
    for start in range(0, MERGE_TM, MERGE_MC):
        rows = slice(start, start + MERGE_MC)
        xp = xp_ref[rows, :]
        if start == 0:
            halo = jnp.where(tile_in_seq > 0, xprev_ref[...], 0.0)
        else:
            halo = xp_ref[start - POOL_HALO:start, :]
        xfull = jnp.concatenate([halo, xp], axis=0)
        t1 = (tile_in_seq * MERGE_TM + start + 1
              + lax.broadcasted_iota(jnp.int32, (MERGE_MC, 1), 0))
        mixed = []
        for gi, w in enumerate(POOL_WINDOWS):
            cols = slice(gi * POOL_GROUP, (gi + 1) * POOL_GROUP)
            s = xfull[:, cols]
            span = 1
            while span < w:
                s = s + pltpu.roll(s, span, axis=0)
                span *= 2
            count = jnp.minimum(t1, w).astype(F32)
            pooled = s[POOL_HALO:] / count - xp[:, cols]
            mg = jnp.dot(pooled.astype(BF16), wpool_ref[gi], preferred_element_type=F32)
            mixed.append((mg * pscale_ref[:, cols]).astype(BF16))
        mixed = jnp.concatenate(mixed, axis=1)
        y_a = jnp.dot(mixed, wa_ref[...], preferred_element_type=F32)
        y_b = jnp.dot(attn_ref[rows, :], wb_ref[...], preferred_element_type=F32)
        merged = (sa_ref[rows, :].astype(F32) * y_a
                  + sb_ref[rows, :].astype(F32) * y_b).astype(BF16)
        y = jnp.dot(merged, wout_ref[...], preferred_element_type=F32)
        z = alpha * x_ref[rows, :] + gate1 * y
        o_ref[rows, :] = _layer_norm(z, g_ref[...], b_ref[...])


def _merge_sublayer(xp, attn, sa, sb, x, gate, ln_g, ln_b, w_pool, pool_scale,
                    w_a, w_b, w_out, *, seq, alpha):
    m = x.shape[0]
    tiles_per_seq = seq // MERGE_TM
    halo_blocks = MERGE_TM // POOL_HALO
    row = lambda i: (i, 0)
    prev_rows = lambda i: (jnp.maximum(i * halo_blocks - 1, 0), 0)
    per_batch = lambda i: (i // tiles_per_seq, 0, 0)
    const = lambda i: (0, 0)
    const3 = lambda i: (0, 0, 0)
    return pl.pallas_call(
        functools.partial(_merge_kernel, alpha=alpha, tiles_per_seq=tiles_per_seq),
        grid=(m // MERGE_TM,),
        in_specs=[
            pl.BlockSpec((MERGE_TM, POOL_WIDTH), row),
            pl.BlockSpec((POOL_HALO, POOL_WIDTH), prev_rows),
            pl.BlockSpec((MERGE_TM, ATTN_WIDTH), row),
            pl.BlockSpec((MERGE_TM, GATE_WIDTH), row),
            pl.BlockSpec((MERGE_TM, GATE_WIDTH), row),
            pl.BlockSpec((MERGE_TM, D_MODEL), row),
            pl.BlockSpec((1, 1, D_MODEL), per_batch),
            pl.BlockSpec((1, D_MODEL), const),
            pl.BlockSpec((1, D_MODEL), const),
            _resident((len(POOL_WINDOWS), POOL_GROUP, POOL_GROUP), const3),
            _resident((1, POOL_WIDTH), const),
            _resident((POOL_WIDTH, D_MODEL), const),
            _resident((ATTN_WIDTH, D_MODEL), const),
            _resident((D_MODEL, D_MODEL), const),
        ],
        out_specs=pl.BlockSpec((MERGE_TM, D_MODEL), row),
        out_shape=jax.ShapeDtypeStruct((m, D_MODEL), F32),
        compiler_params=_params("parallel"),
        name="mix_merge",
    )(xp, xp, attn, sa, sb, x, gate, ln_g, ln_b, w_pool, pool_scale, w_a, w_b, w_out)


def _rope_tables(seq):
    pos = jnp.arange(seq, dtype=F32)
    inv_freq = ROPE_THETA ** (-jnp.arange(0, ROT_DIM, 2, dtype=F32) / ROT_DIM)
    ang = pos[:, None] * inv_freq[None, :]
    cos, sin = jnp.cos(ang), jnp.sin(ang)
    d = np.arange(LANES) % HEAD_DIM
    idx = d % ROT_HALF
    lo = jnp.asarray(d < ROT_HALF)
    hi = jnp.asarray((d >= ROT_HALF) & (d < ROT_DIM))
    cosf = jnp.where(lo | hi, cos[:, idx], 1.0)
    sina = jnp.where(lo, -sin[:, idx], 0.0)
    sinb = jnp.where(hi, sin[:, idx], 0.0)
    return cosf, sina, sinb


def kernel(x, c, w_ada, b_ada, ln_g, ln_b, w_ffn1_in, w_ffn1_out, w_in, b_in, w_pool,
           pool_scale, sinks, w_branch_a, w_branch_b, w_out, w_ffn2_in, w_ffn2_out):
    batch, seq, d = x.shape
    depth = w_ada.shape[0]
    assert d == D_MODEL and batch <= MOD_ROWS
    assert seq % FFN_TM == 0 and seq % PROJ_TM == 0 and seq % MERGE_TM == 0 and seq % WINDOW == 0
    alpha = (2 * depth) ** 0.25
    cosf, sina, sinb = _rope_tables(seq)
    c_pad = jnp.pad(c, ((0, MOD_ROWS - batch), (0, 0)))
    h = x.reshape(batch * seq, d)

    for l in range(depth):
        mod = _modulation(c_pad, w_ada[l], b_ada[l])[:batch]
        mod = mod.reshape(batch, N_SUBLAYERS, 3, 1, d)
        shift = lambda s: mod[:, s, 0]
        scale = lambda s: mod[:, s, 1]
        gate = lambda s: mod[:, s, 2]
        g = lambda s: ln_g[l, s].reshape(1, d)
        b = lambda s: ln_b[l, s].reshape(1, d)

        w_gu, w_d = _ffn_weights(w_ffn1_in[l], w_ffn1_out[l])
        h, w_gu2, w_d2, w_in16 = _ffn_sublayer(
            h, shift(0), scale(0), gate(0), g(0), b(0), w_gu, w_d, seq=seq, alpha=alpha,
            next_weights=(w_ffn2_in[l], w_ffn2_out[l], w_in[l]))

        xp, q, k, v, sa, sb = _input_projection(
            h, shift(1), scale(1), w_in16, b_in[l].reshape(1, IN_WIDTH),
            cosf, sina, sinb, seq=seq)
        attn = _attention(q, k, v, sinks[l], batch=batch, seq=seq)
        h = _merge_sublayer(
            xp, attn, sa, sb, h, gate(1), g(1), b(1), w_pool[l].astype(BF16),
            pool_scale[l].reshape(1, POOL_WIDTH), w_branch_a[l].astype(BF16),
            w_branch_b[l].astype(BF16), w_out[l].astype(BF16), seq=seq, alpha=alpha)

        h = _ffn_sublayer(h, shift(2), scale(2), gate(2), g(2), b(2), w_gu2, w_d2,
                          seq=seq, alpha=alpha)
    return h.reshape(batch, seq, d)
```

```python
import functools

import jax
import jax.numpy as jnp
import numpy as np
from jax import lax
from jax.experimental import pallas as pl
from jax.experimental.pallas import tpu as pltpu

F32 = jnp.float32
BF16 = jnp.bfloat16

D_MODEL = 2048
N_Q_HEADS = 16
N_KV_HEADS = 4
HEAD_DIM = 64
Q_GROUP = N_Q_HEADS // N_KV_HEADS
ATTN_WIDTH = N_Q_HEADS * HEAD_DIM
KV_WIDTH = N_KV_HEADS * HEAD_DIM
WINDOW = 128
ROPE_THETA = 500000.0
ROT_DIM = HEAD_DIM // 4
ROT_HALF = ROT_DIM // 2
POOL_WINDOWS = (2, 4, 8, 16)
POOL_WIDTH = D_MODEL // 2
POOL_GROUP = POOL_WIDTH // len(POOL_WINDOWS)
POOL_HALO = 16
GATE_WIDTH = D_MODEL
IN_WIDTH = POOL_WIDTH + ATTN_WIDTH + 2 * KV_WIDTH + 2 * GATE_WIDTH
D_FF = 5504
N_SUBLAYERS = 3
LN_EPS = 1e-5
MASK_VALUE = -1e30

LANES = 128
BF16_SUBLANES = 16
VMEM_LIMIT = 60 * 1024 * 1024
FFN_TM = 1024
FFN_MC = 512
FFN_EDGE_MC = 256
FFN_TF = 512
D_FF_PAD = ((D_FF + FFN_TF - 1) // FFN_TF) * FFN_TF
CAST_ROWS_IN = 128
CAST_ROWS_OUT = 128
CAST_OUT_PARTS = 4
SIDE_ROWS_IN = 16
SIDE_ROWS_OUT = 32
assert D_FF % LANES == 0 and D_FF % CAST_ROWS_OUT == 0
assert D_FF_PAD % (CAST_ROWS_OUT * CAST_OUT_PARTS) == 0
PROJ_TM = 512
PROJ_MC = 256
PROJ_TN = 512
ATTN_TQ = 512
MERGE_TM = 512
MERGE_MC = 256
ROW_CHUNK = 128
MOD_TN = 1024
MOD_ROWS = 8


def _params(*sem):
    return pltpu.CompilerParams(dimension_semantics=sem, vmem_limit_bytes=VMEM_LIMIT)


def _resident(shape, index_map):
    return pl.BlockSpec(shape, index_map, pipeline_mode=pl.Buffered(1))


def _layer_norm(z, g, b):
    mu = jnp.mean(z, axis=-1, keepdims=True)
    zc = z - mu
    var = jnp.mean(zc * zc, axis=-1, keepdims=True)
    return zc * lax.rsqrt(var + LN_EPS) * g + b


def _mod_kernel(c_ref, w_ref, b_ref, o_ref):
    c = c_ref[...]
    c_act = (c * jax.nn.sigmoid(c)).astype(BF16)
    o_ref[...] = jnp.dot(c_act, w_ref[...].astype(BF16),
                         preferred_element_type=F32) + b_ref[...]


def _modulation(c_pad, w_ada, b_ada):
    n = w_ada.shape[1]
    return pl.pallas_call(
        _mod_kernel,
        grid=(n // MOD_TN,),
        in_specs=[
            pl.BlockSpec((MOD_ROWS, D_MODEL), lambda j: (0, 0)),
            pl.BlockSpec((D_MODEL, MOD_TN), lambda j: (0, j)),
            pl.BlockSpec((1, MOD_TN), lambda j: (0, j)),
        ],
        out_specs=pl.BlockSpec((MOD_ROWS, MOD_TN), lambda j: (0, j)),
        out_shape=jax.ShapeDtypeStruct((MOD_ROWS, n), F32),
        compiler_params=_params("arbitrary"),
        name="adaln_mod",
    )(c_pad, w_ada, b_ada.reshape(1, n))


def _cast_in_kernel(w_ref, o_ref):
    o_ref[0, :, :D_FF] = w_ref[:, :D_FF].astype(BF16)
    o_ref[1, :, :D_FF] = w_ref[:, D_FF:].astype(BF16)
    o_ref[:, :, D_FF:] = jnp.zeros((2, o_ref.shape[1], D_FF_PAD - D_FF), BF16)


def _cast_out_kernel(*refs):
    w_refs, o_ref = refs[:-1], refs[-1]
    j = pl.program_id(0)
    for t, w_ref in enumerate(w_refs):
        rows = slice(t * CAST_ROWS_OUT, (t + 1) * CAST_ROWS_OUT)
        in_range = j * CAST_OUT_PARTS + t < D_FF // CAST_ROWS_OUT

        @pl.when(in_range)
        def _():
            o_ref[rows, :] = w_ref[...].astype(BF16)

        @pl.when(jnp.logical_not(in_range))
        def _():
            o_ref[rows, :] = jnp.zeros((CAST_ROWS_OUT, D_MODEL), BF16)


def _ffn_weights(w_in, w_out):
    w_gu = pl.pallas_call(
        _cast_in_kernel,
        grid=(D_MODEL // CAST_ROWS_IN,),
        in_specs=[pl.BlockSpec((CAST_ROWS_IN, 2 * D_FF), lambda i: (i, 0))],
        out_specs=pl.BlockSpec((2, CAST_ROWS_IN, D_FF_PAD), lambda i: (0, i, 0)),
        out_shape=jax.ShapeDtypeStruct((2, D_MODEL, D_FF_PAD), BF16),
        compiler_params=_params("parallel"),
        name="cast_ffn_in",
    )(w_in)
    last_in = D_FF // CAST_ROWS_OUT - 1
    part = lambda t: pl.BlockSpec(
        (CAST_ROWS_OUT, D_MODEL), lambda j: (jnp.minimum(j * CAST_OUT_PARTS + t, last_in), 0))
    w_d = pl.pallas_call(
        _cast_out_kernel,
        grid=(D_FF_PAD // (CAST_ROWS_OUT * CAST_OUT_PARTS),),
        in_specs=[part(t) for t in range(CAST_OUT_PARTS)],
        out_specs=pl.BlockSpec((CAST_ROWS_OUT * CAST_OUT_PARTS, D_MODEL), lambda j: (j, 0)),
        out_shape=jax.ShapeDtypeStruct((D_FF_PAD, D_MODEL), BF16),
        compiler_params=_params("parallel"),
        name="cast_ffn_out",
    )(*([w_out] * CAST_OUT_PARTS))
    return w_gu, w_d


def _swiglu_chain(u, wg_ref, wu_ref, wd_ref):
    a = jnp.dot(u, wg_ref[...], preferred_element_type=F32)
    b = jnp.dot(u, wu_ref[...], preferred_element_type=F32)
    h = (a * jax.nn.sigmoid(a) * b).astype(BF16)
    return jnp.dot(h, wd_ref[...], preferred_element_type=F32)


def _ffn_kernel(x_ref, shift_ref, scale_ref, gate_ref, g_ref, b_ref, wg_ref, wu_ref, wd_ref,
                *rest, alpha, side_cast):
    j = pl.program_id(1)
    last = pl.num_programs(1) - 1
    if side_cast:
        (next_in_ref, next_out_ref, plain_ref, o_ref,
         cast_in_ref, cast_out_ref, cast_plain_ref, u_scr) = rest
        step = pl.program_id(0) * pl.num_programs(1) + j

        def side_work():
            _cast_in_kernel(next_in_ref, cast_in_ref)
            cast_out_ref[...] = jnp.where(step < D_FF // SIDE_ROWS_OUT,
                                          next_out_ref[...], 0.0).astype(BF16)
            cast_plain_ref[...] = plain_ref[...].astype(BF16)
    else:
        o_ref, u_scr = rest

        def side_work():
            pass

    @pl.when(j == 0)
    def _():
        scale1 = 1.0 + scale_ref[0]
        shift = shift_ref[0]
        for r in range(FFN_TM // FFN_EDGE_MC):
            rows = slice(r * FFN_EDGE_MC, (r + 1) * FFN_EDGE_MC)
            for s in range(r * FFN_EDGE_MC, (r + 1) * FFN_EDGE_MC, ROW_CHUNK):
                sub = slice(s, s + ROW_CHUNK)
                u_scr[sub, :] = (x_ref[sub, :] * scale1 + shift).astype(BF16)
            o_ref[rows, :] = _swiglu_chain(u_scr[rows, :], wg_ref, wu_ref, wd_ref)
        side_work()

    @pl.when((j > 0) & (j < last))
    def _():
        for r in range(FFN_TM // FFN_MC):
            rows = slice(r * FFN_MC, (r + 1) * FFN_MC)
            o_ref[rows, :] += _swiglu_chain(u_scr[rows, :], wg_ref, wu_ref, wd_ref)
        side_work()

    @pl.when(j == last)
    def _():
        gate_half = 0.5 * (1.0 + gate_ref[0])
        g = g_ref[...]
        beta = b_ref[...]

        def finish(r):
            for s in range(r * FFN_EDGE_MC, (r + 1) * FFN_EDGE_MC, ROW_CHUNK):
                sub = slice(s, s + ROW_CHUNK)
                z = alpha * x_ref[sub, :] + gate_half * o_ref[sub, :]
                o_ref[sub, :] = _layer_norm(z, g, beta)

        n_chains = FFN_TM // FFN_EDGE_MC
        for r in range(n_chains):
            rows = slice(r * FFN_EDGE_MC, (r + 1) * FFN_EDGE_MC)
            o_ref[rows, :] += _swiglu_chain(u_scr[rows, :], wg_ref, wu_ref, wd_ref)
            if r > 0:
                finish(r - 1)
        finish(n_chains - 1)
        side_work()


def _ffn_sublayer(x, shift, scale, gate, ln_g, ln_b, w_gu, w_d, *, seq, alpha, next_weights=None):
    m = x.shape[0]
    tiles_per_seq = seq // FFN_TM
    n_tiles, n_chunks = m // FFN_TM, D_FF_PAD // FFN_TF
    row = lambda i, j: (i, 0)
    per_batch = lambda i, j: (i // tiles_per_seq, 0, 0)
    const = lambda i, j: (0, 0)
    in_specs = [
        pl.BlockSpec((FFN_TM, D_MODEL), row),
        pl.BlockSpec((1, 1, D_MODEL), per_batch),
        pl.BlockSpec((1, 1, D_MODEL), per_batch),
        pl.BlockSpec((1, 1, D_MODEL), per_batch),
        pl.BlockSpec((1, D_MODEL), const),
        pl.BlockSpec((1, D_MODEL), const),
        pl.BlockSpec((None, D_MODEL, FFN_TF), lambda i, j: (0, 0, j)),
        pl.BlockSpec((None, D_MODEL, FFN_TF), lambda i, j: (1, 0, j)),
        pl.BlockSpec((FFN_TF, D_MODEL), lambda i, j: (j, 0)),
    ]
    out_specs = [pl.BlockSpec((FFN_TM, D_MODEL), row)]
    out_shape = [jax.ShapeDtypeStruct((m, D_MODEL), F32)]
    args = [x, shift, scale, gate, ln_g, ln_b, w_gu, w_gu, w_d]
    side_cast = next_weights is not None
    if side_cast:
        assert n_tiles * n_chunks * SIDE_ROWS_IN >= D_MODEL
        assert n_tiles * n_chunks * SIDE_ROWS_OUT == D_FF_PAD and D_FF % SIDE_ROWS_OUT == 0
        step = lambda i, j: i * n_chunks + j
        last_in = D_MODEL // SIDE_ROWS_IN - 1
        last_out = D_FF // SIDE_ROWS_OUT - 1
        n_plain = next_weights[2].shape[1]
        in_specs += [
            pl.BlockSpec((SIDE_ROWS_IN, 2 * D_FF), lambda i, j: (jnp.minimum(step(i, j), last_in), 0)),
            pl.BlockSpec((SIDE_ROWS_OUT, D_MODEL), lambda i, j: (jnp.minimum(step(i, j), last_out), 0)),
            pl.BlockSpec((SIDE_ROWS_IN, n_plain), lambda i, j: (jnp.minimum(step(i, j), last_in), 0)),
        ]
        out_specs += [
            pl.BlockSpec((2, SIDE_ROWS_IN, D_FF_PAD),
                         lambda i, j: (0, jnp.minimum(step(i, j), last_in), 0)),
            pl.BlockSpec((SIDE_ROWS_OUT, D_MODEL), lambda i, j: (step(i, j), 0)),
            pl.BlockSpec((SIDE_ROWS_IN, n_plain), lambda i, j: (jnp.minimum(step(i, j), last_in), 0)),
        ]
        out_shape += [jax.ShapeDtypeStruct((2, D_MODEL, D_FF_PAD), BF16),
                      jax.ShapeDtypeStruct((D_FF_PAD, D_MODEL), BF16),
                      jax.ShapeDtypeStruct((D_MODEL, n_plain), BF16)]
        args += list(next_weights)
    outs = pl.pallas_call(
        functools.partial(_ffn_kernel, alpha=alpha, side_cast=side_cast),
        grid=(n_tiles, n_chunks),
        in_specs=in_specs,
        out_specs=out_specs,
        out_shape=out_shape,
        scratch_shapes=[pltpu.VMEM((FFN_TM, D_MODEL), BF16)],
        compiler_params=_params("arbitrary" if side_cast else "parallel", "arbitrary"),
        name="ffn_sublayer",
    )(*args)
    return outs if side_cast else outs[0]


def _rope(t, cosf, sina, sinb):
    up = pltpu.roll(t, LANES - ROT_HALF, axis=1)
    dn = pltpu.roll(t, ROT_HALF, axis=1)
    return t * cosf + up * sina + dn * sinb


def _rope_wide(t, cosf, sina, sinb):
    slabs = [_rope(t[:, s:s + LANES], cosf, sina, sinb)
             for s in range(0, t.shape[1], LANES)]
    return jnp.concatenate(slabs, axis=1)


def _proj_kernel(x_ref, shift_ref, scale_ref, w_ref, b_ref, cos_ref, sina_ref, sinb_ref, *rest):
    n_side = (len(rest) - 7) // 2
    side_in, (xp_ref, q_ref, k_ref, v_ref, sa_ref, sb_ref) = rest[:n_side], rest[n_side:n_side + 6]
    side_out, u_scr = rest[n_side + 6:-1], rest[-1]
    scale1 = 1.0 + scale_ref[0]
    shift = shift_ref[0]
    q_off = POOL_WIDTH
    kv_off = q_off + ATTN_WIDTH
    ga_off = kv_off + 2 * KV_WIDTH
    gb_off = ga_off + GATE_WIDTH
    for start in range(0, PROJ_TM, PROJ_MC):
        rows = slice(start, start + PROJ_MC)
        for s in range(start, start + PROJ_MC, ROW_CHUNK):
            sub = slice(s, s + ROW_CHUNK)
            u_scr[sub, :] = (x_ref[sub, :] * scale1 + shift).astype(BF16)
        u = u_scr[rows, :]
        cosf = cos_ref[rows, :]
        sina = sina_ref[rows, :]
        sinb = sinb_ref[rows, :]
        for col in range(0, IN_WIDTH, PROJ_TN):
            cols = slice(col, col + PROJ_TN)
            h = jnp.dot(u, w_ref[:, cols], preferred_element_type=F32) + b_ref[:, cols]
            if col < q_off:
                xp_ref[rows, cols] = h
            elif col < kv_off:
                q = _rope_wide(h, cosf, sina, sinb) * (HEAD_DIM ** -0.5)
                q_ref[rows, col - q_off:col - q_off + PROJ_TN] = q.astype(BF16)
            elif col < ga_off:
                k_ref[rows, :] = _rope_wide(h[:, :KV_WIDTH], cosf, sina, sinb).astype(BF16)
                v_ref[rows, :] = h[:, KV_WIDTH:].astype(BF16)
            elif col < gb_off:
                sa_ref[rows, col - ga_off:col - ga_off + PROJ_TN] = jax.nn.sigmoid(h).astype(BF16)
            else:
                sb_ref[rows, col - gb_off:col - gb_off + PROJ_TN] = jax.nn.sigmoid(h).astype(BF16)
    for src_ref, dst_ref in zip(side_in, side_out):
        dst_ref[...] = src_ref[...].astype(BF16)


def _input_projection(x, shift, scale, w_in, b_in, cosf, sina, sinb, side_weights, *, seq):
    m = x.shape[0]
    n_tiles = m // PROJ_TM
    tiles_per_seq = seq // PROJ_TM
    side_rows = [w.shape[0] // n_tiles for w in side_weights]
    assert all(r % BF16_SUBLANES == 0 and r * n_tiles == w.shape[0]
               for r, w in zip(side_rows, side_weights))
    side_specs = [pl.BlockSpec((r, w.shape[1]), lambda i: (i, 0))
                  for r, w in zip(side_rows, side_weights)]
    row = lambda i: (i, 0)
    per_batch = lambda i: (i // tiles_per_seq, 0, 0)
    const = lambda i: (0, 0)
    pos = lambda i: (i % tiles_per_seq, 0)
    out_widths = (POOL_WIDTH, ATTN_WIDTH, KV_WIDTH, KV_WIDTH, GATE_WIDTH, GATE_WIDTH)
    out_dtypes = (F32, BF16, BF16, BF16, BF16, BF16)
    return pl.pallas_call(
        _proj_kernel,
        grid=(n_tiles,),
        in_specs=[
            pl.BlockSpec((PROJ_TM, D_MODEL), row),
            pl.BlockSpec((1, 1, D_MODEL), per_batch),
            pl.BlockSpec((1, 1, D_MODEL), per_batch),
            _resident((D_MODEL, IN_WIDTH), const),
            _resident((1, IN_WIDTH), const),
            pl.BlockSpec((PROJ_TM, LANES), pos),
            pl.BlockSpec((PROJ_TM, LANES), pos),
            pl.BlockSpec((PROJ_TM, LANES), pos),
        ] + side_specs,
        out_specs=[pl.BlockSpec((PROJ_TM, w), row) for w in out_widths] + side_specs,
        out_shape=([jax.ShapeDtypeStruct((m, w), dt) for w, dt in zip(out_widths, out_dtypes)]
                   + [jax.ShapeDtypeStruct(w.shape, BF16) for w in side_weights]),
        scratch_shapes=[pltpu.VMEM((PROJ_TM, D_MODEL), BF16)],
        compiler_params=_params("parallel"),
        name="mix_in_proj",
    )(x, shift, scale, w_in, b_in, cosf, sina, sinb, *side_weights)


def _attn_kernel(sinks_ref, q_ref, kp_ref, kc_ref, vp_ref, vc_ref, o_ref):
    has_prev = pl.program_id(1) > 0
    kall = jnp.concatenate([kp_ref[...], kc_ref[...]], axis=0)
    vall = jnp.concatenate([vp_ref[...], vc_ref[...]], axis=0)
    r = lax.broadcasted_iota(jnp.int32, (2 * WINDOW, WINDOW), 0) & (WINDOW - 1)
    c = lax.broadcasted_iota(jnp.int32, (2 * WINDOW, WINDOW), 1)
    own = c <= r
    upper = lax.broadcasted_iota(jnp.int32, (2 * WINDOW, 1), 0) < WINDOW
    lo = lax.broadcasted_iota(jnp.int32, (2 * WINDOW, LANES), 1) < HEAD_DIM
    zero = jnp.zeros((2 * WINDOW, LANES), BF16)
    contract_last = (((1,), (1,)), ((), ()))
    for blk in range(ATTN_TQ // WINDOW):
        rows = slice(blk * WINDOW, (blk + 1) * WINDOW)
        keys = slice(blk * WINDOW, (blk + 2) * WINDOW)
        scores, values = [], []
        for slab in range(KV_WIDTH // LANES):
            lanes = slice(slab * LANES, (slab + 1) * LANES)
            ks, vs = kall[keys, lanes], vall[keys, lanes]
            ks_sw = pltpu.roll(ks, HEAD_DIM, axis=1)
            vs_sw = pltpu.roll(vs, HEAD_DIM, axis=1)
            for half in range(LANES // HEAD_DIM):
                kv = slab * (LANES // HEAD_DIM) + half
                if half == 0:
                    k_lo, k_hi = jnp.where(lo, ks, zero), jnp.where(lo, zero, ks_sw)
                    v_lo, v_hi = jnp.where(lo, vs, zero), jnp.where(lo, zero, vs_sw)
                else:
                    k_lo, k_hi = jnp.where(lo, ks_sw, zero), jnp.where(lo, zero, ks)
                    v_lo, v_hi = jnp.where(lo, vs_sw, zero), jnp.where(lo, zero, vs)
                base = kv * Q_GROUP * HEAD_DIM
                q2 = jnp.concatenate([q_ref[rows, base:base + LANES],
                                      q_ref[rows, base + LANES:base + 2 * LANES]], axis=0)
                for k_x, v_x in ((k_lo, v_lo), (k_hi, v_hi)):
                    scores.append(lax.dot_general(q2, k_x, contract_last,
                                                  preferred_element_type=F32))
                    values.append(v_x)
        probs = []
        for t, s in enumerate(scores):
            head = (t // 2) * Q_GROUP + (t % 2)
            s_prev, s_own = s[:, :WINDOW], s[:, WINDOW:]
            if blk == 0:
                s_prev = jnp.where(has_prev, s_prev, MASK_VALUE)
            dense = jnp.where(own, s_own, s_prev)
            sink = jnp.where(upper, sinks_ref[head], sinks_ref[head + 2])
            m = jnp.maximum(jnp.max(dense, axis=-1, keepdims=True), sink)
            p = jnp.exp(dense - m)
            denom = jnp.sum(p, axis=-1, keepdims=True) + jnp.exp(sink - m)
            p = (p * (1.0 / denom)).astype(BF16)
            pz = jnp.zeros_like(p)
            probs.append(jnp.concatenate([jnp.where(own, pz, p), jnp.where(own, p, pz)], axis=1))
        for kv in range(N_KV_HEADS):
            base = kv * Q_GROUP * HEAD_DIM
            out = (jnp.dot(probs[2 * kv], values[2 * kv], preferred_element_type=F32)
                   + jnp.dot(probs[2 * kv + 1], values[2 * kv + 1], preferred_element_type=F32))
            out = out.astype(BF16)
            o_ref[rows, base:base + LANES] = out[:WINDOW]
            o_ref[rows, base + LANES:base + 2 * LANES] = out[WINDOW:]


def _attention(q, k, v, sinks, *, batch, seq):
    m = q.shape[0]
    nt = seq // ATTN_TQ
    halo_blocks = ATTN_TQ // WINDOW
    cur = lambda b, i: (b * nt + i, 0)
    prev = lambda b, i: (jnp.maximum((b * nt + i) * halo_blocks - 1, 0), 0)
    return pl.pallas_call(
        _attn_kernel,
        grid=(batch, nt),
        in_specs=[
            pl.BlockSpec(memory_space=pltpu.SMEM),
            pl.BlockSpec((ATTN_TQ, ATTN_WIDTH), cur),
            pl.BlockSpec((WINDOW, KV_WIDTH), prev),
            pl.BlockSpec((ATTN_TQ, KV_WIDTH), cur),
            pl.BlockSpec((WINDOW, KV_WIDTH), prev),
            pl.BlockSpec((ATTN_TQ, KV_WIDTH), cur),
        ],
        out_specs=pl.BlockSpec((ATTN_TQ, ATTN_WIDTH), cur),
        out_shape=jax.ShapeDtypeStruct((m, ATTN_WIDTH), BF16),
        compiler_params=_params("parallel", "parallel"),
        name="swa_attention",
    )(sinks, q, k, k, v, v)


def _merge_kernel(xp_ref, xprev_ref, attn_ref, sa_ref, sb_ref, x_ref, gate_ref, g_ref, b_ref,
                  wpool_ref, pscale_ref, wa_ref, wb_ref, wout_ref, o_ref, *, alpha, tiles_per_seq):
    tile_in_seq = pl.program_id(0) % tiles_per_seq
    gate1 = 1.0 + gate_ref[0]
    for start in range(0, MERGE_TM, MERGE_MC):
        rows = slice(start, start + MERGE_MC)
        xp = xp_ref[rows, :]
        if start == 0:
            halo = jnp.where(tile_in_seq > 0, xprev_ref[...], 0.0)
        else:
            halo = xp_ref[start - POOL_HALO:start, :]
        xfull = jnp.concatenate([halo, xp], axis=0)
        t1 = (tile_in_seq * MERGE_TM + start + 1
              + lax.broadcasted_iota(jnp.int32, (MERGE_MC, 1), 0))
        mixed = []
        for gi, w in enumerate(POOL_WINDOWS):
            cols = slice(gi * POOL_GROUP, (gi + 1) * POOL_GROUP)
            s = xfull[:, cols]
            span = 1
            while span < w:
                s = s + pltpu.roll(s, span, axis=0)
                span *= 2
            count = jnp.minimum(t1, w).astype(F32)
            pooled = s[POOL_HALO:] / count - xp[:, cols]
            mg = jnp.dot(pooled.astype(BF16), wpool_ref[gi], preferred_element_type=F32)
            mixed.append((mg * pscale_ref[:, cols]).astype(BF16))
        mixed = jnp.concatenate(mixed, axis=1)
        y_a = jnp.dot(mixed, wa_ref[...], preferred_element_type=F32)
        y_b = jnp.dot(attn_ref[rows, :], wb_ref[...], preferred_element_type=F32)
        merged = (sa_ref[rows, :].astype(F32) * y_a
                  + sb_ref[rows, :].astype(F32) * y_b).astype(BF16)
        y = jnp.dot(merged, wout_ref[...], preferred_element_type=F32)
        z = alpha * x_ref[rows, :] + gate1 * y
        o_ref[rows, :] = _layer_norm(z, g_ref[...], b_ref[...])


def _merge_sublayer(xp, attn, sa, sb, x, gate, ln_g, ln_b, w_pool, pool_scale,
                    w_a, w_b, w_out, *, seq, alpha):
    m = x.shape[0]
    tiles_per_seq = seq // MERGE_TM
    halo_blocks = MERGE_TM // POOL_HALO
    row = lambda i: (i, 0)
    prev_rows = lambda i: (jnp.maximum(i * halo_blocks - 1, 0), 0)
    per_batch = lambda i: (i // tiles_per_seq, 0, 0)
    const = lambda i: (0, 0)
    const3 = lambda i: (0, 0, 0)
    return pl.pallas_call(
        functools.partial(_merge_kernel, alpha=alpha, tiles_per_seq=tiles_per_seq),
        grid=(m // MERGE_TM,),
        in_specs=[
            pl.BlockSpec((MERGE_TM, POOL_WIDTH), row),
            pl.BlockSpec((POOL_HALO, POOL_WIDTH), prev_rows),
            pl.BlockSpec((MERGE_TM, ATTN_WIDTH), row),
            pl.BlockSpec((MERGE_TM, GATE_WIDTH), row),
            pl.BlockSpec((MERGE_TM, GATE_WIDTH), row),
            pl.BlockSpec((MERGE_TM, D_MODEL), row),
            pl.BlockSpec((1, 1, D_MODEL), per_batch),
            pl.BlockSpec((1, D_MODEL), const),
            pl.BlockSpec((1, D_MODEL), const),
            _resident((len(POOL_WINDOWS), POOL_GROUP, POOL_GROUP), const3),
            _resident((1, POOL_WIDTH), const),
            _resident((POOL_WIDTH, D_MODEL), const),
            _resident((ATTN_WIDTH, D_MODEL), const),
            _resident((D_MODEL, D_MODEL), const),
        ],
        out_specs=pl.BlockSpec((MERGE_TM, D_MODEL), row),
        out_shape=jax.ShapeDtypeStruct((m, D_MODEL), F32),
        compiler_params=_params("parallel"),
        name="mix_merge",
    )(xp, xp, attn, sa, sb, x, gate, ln_g, ln_b, w_pool, pool_scale, w_a, w_b, w_out)


def _rope_tables(seq):
    pos = jnp.arange(seq, dtype=F32)
    inv_freq = ROPE_THETA ** (-jnp.arange(0, ROT_DIM, 2, dtype=F32) / ROT_DIM)
    ang = pos[:, None] * inv_freq[None, :]
    cos, sin = jnp.cos(ang), jnp.sin(ang)
    d = np.arange(LANES) % HEAD_DIM
    idx = d % ROT_HALF
    lo = jnp.asarray(d < ROT_HALF)
    hi = jnp.asarray((d >= ROT_HALF) & (d < ROT_DIM))
    cosf = jnp.where(lo | hi, cos[:, idx], 1.0)
    sina = jnp.where(lo, -sin[:, idx], 0.0)
    sinb = jnp.where(hi, sin[:, idx], 0.0)
    return cosf, sina, sinb


def kernel(x, c, w_ada, b_ada, ln_g, ln_b, w_ffn1_in, w_ffn1_out, w_in, b_in, w_pool,
           pool_scale, sinks, w_branch_a, w_branch_b, w_out, w_ffn2_in, w_ffn2_out):
    batch, seq, d = x.shape
    depth = w_ada.shape[0]
    assert d == D_MODEL and batch <= MOD_ROWS
    assert seq % FFN_TM == 0 and seq % PROJ_TM == 0 and seq % MERGE_TM == 0 and seq % WINDOW == 0
    alpha = (2 * depth) ** 0.25
    cosf, sina, sinb = _rope_tables(seq)
    c_pad = jnp.pad(c, ((0, MOD_ROWS - batch), (0, 0)))
    h = x.reshape(batch * seq, d)

    for l in range(depth):
        mod = _modulation(c_pad, w_ada[l], b_ada[l])[:batch]
        mod = mod.reshape(batch, N_SUBLAYERS, 3, 1, d)
        shift = lambda s: mod[:, s, 0]
        scale = lambda s: mod[:, s, 1]
        gate = lambda s: mod[:, s, 2]
        g = lambda s: ln_g[l, s].reshape(1, d)
        b = lambda s: ln_b[l, s].reshape(1, d)

        w_gu, w_d = _ffn_weights(w_ffn1_in[l], w_ffn1_out[l])
        h, w_gu2, w_d2, w_in16 = _ffn_sublayer(
            h, shift(0), scale(0), gate(0), g(0), b(0), w_gu, w_d, seq=seq, alpha=alpha,
            next_weights=(w_ffn2_in[l], w_ffn2_out[l], w_in[l]))

        xp, q, k, v, sa, sb, w_a16, w_b16, w_out16 = _input_projection(
            h, shift(1), scale(1), w_in16, b_in[l].reshape(1, IN_WIDTH),
            cosf, sina, sinb, (w_branch_a[l], w_branch_b[l], w_out[l]), seq=seq)
        attn = _attention(q, k, v, sinks[l], batch=batch, seq=seq)
        h = _merge_sublayer(
            xp, attn, sa, sb, h, gate(1), g(1), b(1), w_pool[l].astype(BF16),
            pool_scale[l].reshape(1, POOL_WIDTH), w_a16, w_b16, w_out16, seq=seq, alpha=alpha)

        h = _ffn_sublayer(h, shift(2), scale(2), gate(2), g(2), b(2), w_gu2, w_d2,
                          seq=seq, alpha=alpha)
    return h.reshape(batch, seq, d)
```
